```python
import math
import jax, jax.numpy as jnp
from jax import lax
import numpy as np

D_MODEL = 1024
BATCH = 8
SEQ = 4096
DEPTH = 1

CHUNK = 64
MEM_LEN = 256
LN_EPS = 1e-5
SB_HEADS = 16
SB_HEAD_DIM = 64
SB_WIDTH = SB_HEADS * SB_HEAD_DIM
SB_BLOCK = 128
SSD_EXPAND = 2
SSD_INNER = SSD_EXPAND * D_MODEL
SSD_HEAD_DIM = 64
SSD_HEADS = SSD_INNER // SSD_HEAD_DIM
SSD_GROUPS = 4
SSD_HEADS_PER_GROUP = SSD_HEADS // SSD_GROUPS
SSD_STATE = 128
SSD_CONV = 4
SSD_CONV_CH = SSD_INNER + 2 * SSD_GROUPS * SSD_STATE
N_BRANCH = 2
IN_WIDTH = 3 * SB_WIDTH + SSD_INNER + SSD_CONV_CH + SSD_HEADS + N_BRANCH * D_MODEL
X_HEADS = 4
X_HEAD_DIM = D_MODEL // X_HEADS
N_EXPERTS = 32
TOP_K = 4
D_EXPERT = D_MODEL
SWIGLU_LIMIT = 7.0
SWIGLU_ALPHA = 1.702
EXPERT_BLOCK = 128
DN_ALPHA = (2 * DEPTH) ** 0.25
DN_BETA = (8 * DEPTH) ** -0.25

kernel_name = 'hybrid_stickbreak_ssd_moe_deepnorm'


def layer_norm(x, g, b):
    xf = x.astype(jnp.float32)
    mu = jnp.mean(xf, axis=-1, keepdims=True)
    var = jnp.mean(jnp.square(xf - mu), axis=-1, keepdims=True)
    return ((xf - mu) * lax.rsqrt(var + LN_EPS)).astype(x.dtype) * g + b


def stick_breaking_attention(q, k, v):
    S = q.shape[1]
    scale = SB_HEAD_DIM ** -0.5
    outs = []
    for start in range(0, S, SB_BLOCK):
        end = start + SB_BLOCK
        qb = q[:, start:end]
        kb = k[:, :end]
        vb = v[:, :end]
        z = jnp.einsum('bqhd,bkhd->bhqk', qb, kb, preferred_element_type=jnp.float32) * scale
        t_idx = start + jnp.arange(SB_BLOCK)
        s_idx = jnp.arange(end)
        mask = s_idx[None, :] < t_idx[:, None]
        log_1m = jnp.where(mask, jax.nn.log_sigmoid(-z), 0.0)
        rev = lax.cumsum(log_1m, axis=3, reverse=True)
        after = jnp.pad(rev[..., 1:], ((0, 0), (0, 0), (0, 0), (0, 1)))
        w = jnp.where(mask, jnp.exp(jax.nn.log_sigmoid(z) + after), 0.0)
        outs.append(jnp.einsum('bhqk,bkhd->bqhd', w.astype(v.dtype), vb))
    return jnp.concatenate(outs, axis=1)


def causal_depthwise_conv(u, w, b):
    out = lax.conv_general_dilated(
        u, w[:, None, :].astype(u.dtype), window_strides=(1,), padding=[(SSD_CONV - 1, 0)],
        dimension_numbers=('NWC', 'WIO', 'NWC'), feature_group_count=u.shape[-1])
    return out + b


def ssd_chunked_scan(xdt, da, bm, cm):
    bsz, S = xdt.shape[:2]
    nc = S // CHUNK

    def to_chunks(u):
        u = u.reshape((bsz, nc, CHUNK) + u.shape[2:])
        return jnp.moveaxis(u, 1, 0)

    causal = jnp.tril(jnp.ones((CHUNK, CHUNK), dtype=bool))[None, :, :, None, None]

    def step(h, inp):
        x_c, da_c, b_c, c_c = inp
        cs = jnp.cumsum(da_c, axis=1)
        diff = cs[:, :, None] - cs[:, None, :]
        seg = jnp.exp(jnp.where(causal, diff, -jnp.inf))
        cb = jnp.einsum('btgn,bsgn->btsg', c_c, b_c)
        y = jnp.einsum('btsg,btsgr,bsgrp->btgrp', cb, seg, x_c)
        y = y + jnp.einsum('btgn,bgrpn->btgrp', c_c, h) * jnp.exp(cs)[..., None]
        decay_end = jnp.exp(cs[:, -1:] - cs)
        h = h * jnp.exp(cs[:, -1])[..., None, None] + jnp.einsum('bsgn,bsgr,bsgrp->bgrpn', b_c, decay_end, x_c)
        return h, y

    h0 = jnp.zeros((bsz, SSD_GROUPS, SSD_HEADS_PER_GROUP, SSD_HEAD_DIM, SSD_STATE), jnp.float32)
    _, ys = lax.scan(step, h0, (to_chunks(xdt), to_chunks(da), to_chunks(bm), to_chunks(cm)))
    return jnp.moveaxis(ys, 0, 1).reshape(xdt.shape)


def hybrid_mixer(h, w_in, b_branch_gate, conv_w, conv_b, dt_bias, a_log, d_skip, ssd_norm_g,
                 w_sb, w_ssd, w_mix_out):
    bsz, S, _ = h.shape
    splits = list(np.cumsum([SB_WIDTH, SB_WIDTH, SB_WIDTH, SSD_INNER, SSD_CONV_CH, SSD_HEADS]))
    q, k, v, z, xbc, dt_raw, gate_logits = jnp.split(h @ w_in, splits, axis=-1)
    hs = (bsz, S, SB_HEADS, SB_HEAD_DIM)
    o_sb = stick_breaking_attention(q.reshape(hs), k.reshape(hs), v.reshape(hs)).reshape(bsz, S, SB_WIDTH)
    xbc = jax.nn.silu(causal_depthwise_conv(xbc, conv_w, conv_b))
    xs, bm, cm = jnp.split(xbc, [SSD_INNER, SSD_INNER + SSD_GROUPS * SSD_STATE], axis=-1)
    xh = xs.reshape(bsz, S, SSD_GROUPS, SSD_HEADS_PER_GROUP, SSD_HEAD_DIM).astype(jnp.float32)
    bm = bm.reshape(bsz, S, SSD_GROUPS, SSD_STATE).astype(jnp.float32)
    cm = cm.reshape(bsz, S, SSD_GROUPS, SSD_STATE).astype(jnp.float32)
    dt = jax.nn.softplus(dt_raw.astype(jnp.float32) + dt_bias.astype(jnp.float32))
    dt = dt.reshape(bsz, S, SSD_GROUPS, SSD_HEADS_PER_GROUP)
    a = -jnp.exp(a_log.astype(jnp.float32)).reshape(SSD_GROUPS, SSD_HEADS_PER_GROUP)
    y = ssd_chunked_scan(xh * dt[..., None], dt * a, bm, cm)
    y = y + d_skip.astype(jnp.float32).reshape(SSD_GROUPS, SSD_HEADS_PER_GROUP)[..., None] * xh
    y = y.reshape(bsz, S, SSD_INNER) * jax.nn.silu(z.astype(jnp.float32))
    yg = y.reshape(bsz, S, SSD_GROUPS, SSD_INNER // SSD_GROUPS)
    yg = yg * lax.rsqrt(jnp.mean(jnp.square(yg), axis=-1, keepdims=True) + LN_EPS)
    o_ssd = yg.reshape(bsz, S, SSD_INNER).astype(h.dtype) * ssd_norm_g
    gates = jax.nn.sigmoid(gate_logits.reshape(bsz, S, N_BRANCH, D_MODEL) + b_branch_gate)
    merged = gates[:, :, 0] * (o_sb @ w_sb) + gates[:, :, 1] * (o_ssd @ w_ssd)
    return merged @ w_mix_out


def memory_cross_attention(h, mem, w_xq, w_xk, w_xv, w_xo):
    bsz, S, _ = h.shape
    q = (h @ w_xq).reshape(bsz, S, X_HEADS, X_HEAD_DIM)
    k = (mem @ w_xk).reshape(bsz, mem.shape[1], X_HEADS, X_HEAD_DIM)
    v = (mem @ w_xv).reshape(bsz, mem.shape[1], X_HEADS, X_HEAD_DIM)
    s = jnp.einsum('bqhd,bmhd->bhqm', q, k, preferred_element_type=jnp.float32) * (X_HEAD_DIM ** -0.5)
    p = jax.nn.softmax(s, axis=-1).astype(v.dtype)
    o = jnp.einsum('bhqm,bmhd->bqhd', p, v).reshape(bsz, S, D_MODEL)
    return o @ w_xo


def moe_ffn(h, w_router, b_router, w_e_gate, b_e_gate, w_e_up, b_e_up, w_e_down, b_e_down):
    bsz, S, D = h.shape
    T = bsz * S
    TK = T * TOP_K
    xt = h.reshape(T, D)
    logits = (xt @ w_router).astype(jnp.float32) + b_router.astype(jnp.float32)
    top_vals, top_idx = lax.top_k(logits, TOP_K)
    probs = jax.nn.softmax(top_vals, axis=-1)
    flat_e = top_idx.reshape(TK)
    flat_tok = jnp.arange(TK, dtype=jnp.int32) // TOP_K
    flat_p = probs.reshape(TK)
    order = jnp.argsort(flat_e)
    se = flat_e[order]
    counts = jnp.bincount(flat_e, length=N_EXPERTS)
    padded = (counts + EXPERT_BLOCK - 1) // EXPERT_BLOCK * EXPERT_BLOCK
    start_sorted = jnp.cumsum(counts) - counts
    end_padded = jnp.cumsum(padded)
    start_padded = end_padded - padded
    dest = start_padded[se] + jnp.arange(TK, dtype=jnp.int32) - start_sorted[se]
    n_blocks = -(-(TK + N_EXPERTS * (EXPERT_BLOCK - 1)) // EXPERT_BLOCK)
    n_rows = n_blocks * EXPERT_BLOCK
    row_tok = jnp.full((n_rows,), T, dtype=jnp.int32).at[dest].set(flat_tok[order])
    row_p = jnp.zeros((n_rows,), jnp.float32).at[dest].set(flat_p[order])
    block_e = jnp.minimum(jnp.searchsorted(end_padded, jnp.arange(n_blocks) * EXPERT_BLOCK, side='right'),
                          N_EXPERTS - 1)
    x_rows = jnp.concatenate([xt, jnp.zeros((1, D), xt.dtype)], axis=0)[row_tok]
    x_rows = x_rows.reshape(n_blocks, EXPERT_BLOCK, D)

    def expert_block(args):
        xb, e = args
        g = jnp.minimum(xb @ w_e_gate[e] + b_e_gate[e], SWIGLU_LIMIT)
        u = jnp.clip(xb @ w_e_up[e] + b_e_up[e], -SWIGLU_LIMIT, SWIGLU_LIMIT)
        act = (u + 1.0) * g * jax.nn.sigmoid(SWIGLU_ALPHA * g)
        return act @ w_e_down[e] + b_e_down[e]

    y_rows = lax.map(expert_block, (x_rows, block_e)).reshape(n_rows, D)
    y = jax.ops.segment_sum(y_rows * row_p[:, None].astype(y_rows.dtype), row_tok, num_segments=T + 1)[:T]
    return y.reshape(bsz, S, D)


def setup_inputs(seed: int = 0) -> dict:
    key = jax.random.key(seed)
    keys = iter(jax.random.split(key, 40))

    def nrm(shape, scale):
        return jax.random.normal(next(keys), shape, jnp.float32) * scale

    L = DEPTH
    dt0 = jnp.exp(jax.random.uniform(next(keys), (L, SSD_HEADS), jnp.float32, math.log(1e-3), math.log(1e-1)))
    dt_bias = dt0 + jnp.log(-jnp.expm1(-dt0))
    a_log = jnp.log(jax.random.uniform(next(keys), (L, SSD_HEADS), jnp.float32, 1.0, 16.0))
    return {
        'x': nrm((BATCH, SEQ, D_MODEL), 1.0),
        'mem': nrm((BATCH, MEM_LEN, D_MODEL), 1.0),
        'ln_in_g': 1.0 + nrm((D_MODEL,), 0.02),
        'ln_in_b': nrm((D_MODEL,), 0.02),
        'w_in': nrm((L, D_MODEL, IN_WIDTH), D_MODEL ** -0.5),
        'b_branch_gate': nrm((L, N_BRANCH, D_MODEL), 0.02),
        'conv_w': nrm((L, SSD_CONV, SSD_CONV_CH), SSD_CONV ** -0.5),
        'conv_b': nrm((L, SSD_CONV_CH), 0.02),
        'dt_bias': dt_bias,
        'a_log': a_log,
        'd_skip': 1.0 + nrm((L, SSD_HEADS), 0.1),
        'ssd_norm_g': 1.0 + nrm((L, SSD_INNER), 0.02),
        'w_sb': nrm((L, SB_WIDTH, D_MODEL), SB_WIDTH ** -0.5),
        'w_ssd': nrm((L, SSD_INNER, D_MODEL), SSD_INNER ** -0.5),
        'w_mix_out': nrm((L, D_MODEL, D_MODEL), D_MODEL ** -0.5 * DN_BETA),
        'ln1_g': 1.0 + nrm((L, D_MODEL), 0.02),
        'ln1_b': nrm((L, D_MODEL), 0.02),
        'w_xq': nrm((L, D_MODEL, D_MODEL), D_MODEL ** -0.5),
        'w_xk': nrm((L, D_MODEL, D_MODEL), D_MODEL ** -0.5),
        'w_xv': nrm((L, D_MODEL, D_MODEL), D_MODEL ** -0.5 * DN_BETA),
        'w_xo': nrm((L, D_MODEL, D_MODEL), D_MODEL ** -0.5 * DN_BETA),
        'ln2_g': 1.0 + nrm((L, D_MODEL), 0.02),
        'ln2_b': nrm((L, D_MODEL), 0.02),
        'w_router': nrm((L, D_MODEL, N_EXPERTS), D_MODEL ** -0.5),
        'b_router': nrm((L, N_EXPERTS), 0.01),
        'w_e_gate': nrm((L, N_EXPERTS, D_MODEL, D_EXPERT), D_MODEL ** -0.5),
        'b_e_gate': nrm((L, N_EXPERTS, D_EXPERT), 0.01),
        'w_e_up': nrm((L, N_EXPERTS, D_MODEL, D_EXPERT), D_MODEL ** -0.5),
        'b_e_up': nrm((L, N_EXPERTS, D_EXPERT), 0.01),
        'w_e_down': nrm((L, N_EXPERTS, D_EXPERT, D_MODEL), D_EXPERT ** -0.5 * DN_BETA),
        'b_e_down': nrm((L, N_EXPERTS, D_MODEL), 0.01),
        'ln3_g': 1.0 + nrm((L, D_MODEL), 0.02),
        'ln3_b': nrm((L, D_MODEL), 0.02),
    }


def reference(x, mem, ln_in_g, ln_in_b, w_in, b_branch_gate, conv_w, conv_b, dt_bias, a_log, d_skip,
              ssd_norm_g, w_sb, w_ssd, w_mix_out, ln1_g, ln1_b, w_xq, w_xk, w_xv, w_xo, ln2_g, ln2_b,
              w_router, b_router, w_e_gate, b_e_gate, w_e_up, b_e_up, w_e_down, b_e_down, ln3_g, ln3_b):
    h = layer_norm(x, ln_in_g, ln_in_b)
    for l in range(DEPTH):
        mix = hybrid_mixer(h, w_in[l], b_branch_gate[l], conv_w[l], conv_b[l], dt_bias[l], a_log[l],
                           d_skip[l], ssd_norm_g[l], w_sb[l], w_ssd[l], w_mix_out[l])
        h = layer_norm(DN_ALPHA * h + mix, ln1_g[l], ln1_b[l])
        xa = memory_cross_attention(h, mem, w_xq[l], w_xk[l], w_xv[l], w_xo[l])
        h = layer_norm(DN_ALPHA * h + xa, ln2_g[l], ln2_b[l])
        ff = moe_ffn(h, w_router[l], b_router[l], w_e_gate[l], b_e_gate[l], w_e_up[l], b_e_up[l],
                     w_e_down[l], b_e_down[l])
        h = layer_norm(DN_ALPHA * h + ff, ln3_g[l], ln3_b[l])
    return h
```

```python
import functools

import jax
import jax.numpy as jnp
from jax import lax
from jax.experimental import pallas as pl
from jax.experimental.pallas import tpu as pltpu

F32 = jnp.float32
BF16 = jnp.bfloat16
HIGHEST = lax.Precision.HIGHEST

D_MODEL = 1024
LN_EPS = 1e-5
DN_ALPHA = 2.0 ** 0.25
LANES = 128

SB_HEADS = 16
SB_HEAD_DIM = 64
SB_TILE = 256

SSD_INNER = 2048
SSD_HEADS = 32
SSD_HEAD_DIM = 64
SSD_GROUPS = 4
SSD_GROUP_WIDTH = SSD_INNER // SSD_GROUPS
SSD_STATE = 128
SSD_CONV = 4
SSD_CONV_CH = SSD_INNER + 2 * SSD_GROUPS * SSD_STATE
SSD_CHUNK = 128

X_HEADS = 4
X_HEAD_DIM = 256
MEM_LEN = 256

N_EXPERTS = 32
TOP_K = 4
SWIGLU_LIMIT = 7.0
SWIGLU_ALPHA = 1.702
EXPERT_ROWS = 256

VMEM_LIMIT = 52 * 1024 * 1024


def _cparams(*sem):
    return pltpu.CompilerParams(dimension_semantics=sem, vmem_limit_bytes=VMEM_LIMIT)


def _layer_norm(x, g, b):
    mu = jnp.mean(x, axis=-1, keepdims=True)
    xc = x - mu
    var = jnp.mean(xc * xc, axis=-1, keepdims=True)
    return xc * lax.rsqrt(var + LN_EPS) * g + b


def _dot(a, b, precision=None):
    return jnp.dot(a, b, preferred_element_type=F32, precision=precision)


def _dot_nt(a, b, precision=None):
    return lax.dot_general(a, b, (((1,), (1,)), ((), ())), preferred_element_type=F32, precision=precision)


_N_QKV, _N_Z, _N_XBC, _N_GATE = 3, 2, 3, 2


def _inproj_kernel(x_ref, g_ref, b_ref, w_ref, wdt_ref, h_ref, qkv_ref, z_ref, xbc_ref, gate_ref, dt_ref, hb_ref):
    j = pl.program_id(1)

    @pl.when(j == 0)
    def _():
        h = _layer_norm(x_ref[...], g_ref[...], b_ref[...])
        h_ref[...] = h
        hb = h.astype(BF16)
        hb_ref[...] = hb
        dt_ref[...] = _dot(hb, wdt_ref[...])

    acc = _dot(hb_ref[...], w_ref[...])

    @pl.when(j < _N_QKV)
    def _():
        for c in range(D_MODEL // LANES):
            qkv_ref[0, c] = acc[:, c * LANES:(c + 1) * LANES].astype(BF16)

    @pl.when((j >= _N_QKV) & (j < _N_QKV + _N_Z))
    def _():
        z_ref[...] = acc

    @pl.when((j >= _N_QKV + _N_Z) & (j < _N_QKV + _N_Z + _N_XBC))
    def _():
        xbc_ref[...] = acc

    @pl.when(j >= _N_QKV + _N_Z + _N_XBC)
    def _():
        gate_ref[...] = acc


def _inproj(x2, ln_g, ln_b, w_main, w_dt, tm=512):
    T = x2.shape[0]
    nj = _N_QKV + _N_Z + _N_XBC + _N_GATE
    o_z, o_xbc, o_gate = _N_QKV, _N_QKV + _N_Z, _N_QKV + _N_Z + _N_XBC
    return pl.pallas_call(
        _inproj_kernel,
        grid=(T // tm, nj),
        in_specs=[
            pl.BlockSpec((tm, D_MODEL), lambda i, j: (i, 0)),
            pl.BlockSpec((1, D_MODEL), lambda i, j: (0, 0)),
            pl.BlockSpec((1, D_MODEL), lambda i, j: (0, 0)),
            pl.BlockSpec((D_MODEL, D_MODEL), lambda i, j: (0, j)),
            pl.BlockSpec((D_MODEL, LANES), lambda i, j: (0, 0)),
        ],
        out_specs=[
            pl.BlockSpec((tm, D_MODEL), lambda i, j: (i, 0)),
            pl.BlockSpec((1, D_MODEL // LANES, tm, LANES), lambda i, j: (jnp.minimum(j, _N_QKV - 1), 0, i, 0)),
            pl.BlockSpec((tm, D_MODEL), lambda i, j: (i, jnp.clip(j - o_z, 0, _N_Z - 1))),
            pl.BlockSpec((tm, D_MODEL), lambda i, j: (i, jnp.clip(j - o_xbc, 0, _N_XBC - 1))),
            pl.BlockSpec((tm, D_MODEL), lambda i, j: (i, jnp.clip(j - o_gate, 0, _N_GATE - 1))),
            pl.BlockSpec((tm, LANES), lambda i, j: (i, 0)),
        ],
        out_shape=[
            jax.ShapeDtypeStruct((T, D_MODEL), F32),
            jax.ShapeDtypeStruct((3, D_MODEL // LANES, T, LANES), BF16),
            jax.ShapeDtypeStruct((T, _N_Z * D_MODEL), F32),
            jax.ShapeDtypeStruct((T, _N_XBC * D_MODEL), F32),
            jax.ShapeDtypeStruct((T, _N_GATE * D_MODEL), F32),
            jax.ShapeDtypeStruct((T, LANES), F32),
        ],
        scratch_shapes=[pltpu.VMEM((tm, D_MODEL), BF16)],
        compiler_params=_cparams("parallel", "arbitrary"),
        name="inproj",
    )(x2, ln_g, ln_b, w_main, w_dt)


def _sb_kernel(q_ref, k_ref, v_ref, o_ref):
    t = SB_TILE
    i = pl.program_id(2)
    lane = lax.broadcasted_iota(jnp.int32, (t, LANES), 1)
    qs = q_ref[0, 0] * BF16(SB_HEAD_DIM ** -0.5)
    zero = jnp.zeros_like(qs)
    q_heads = (jnp.where(lane < SB_HEAD_DIM, qs, zero), jnp.where(lane >= SB_HEAD_DIM, qs, zero))
    row = lax.broadcasted_iota(jnp.int32, (t, t), 0)
    col = lax.broadcasted_iota(jnp.int32, (t, t), 1)
    later = (row > col).astype(BF16)
    causal = col < row

    def block(jb, carry, diag):
        k = k_ref[0, 0, pl.ds(pl.multiple_of(jb * t, t), t), :]
        v = v_ref[0, 0, pl.ds(pl.multiple_of(jb * t, t), t), :]
        new = []
        for hd in range(2):
            c, acc = carry[hd]
            z = _dot_nt(q_heads[hd], k)
            sp = jnp.maximum(z, 0.0) + jnp.log(1.0 + jnp.exp(-jnp.abs(z)))
            spm = jnp.where(causal, sp, 0.0) if diag else sp
            hi = spm.astype(BF16)
            lo = (spm - hi.astype(F32)).astype(BF16)
            after = _dot(hi, later) + _dot(lo, later)
            w = jnp.exp(z - sp - after - c)
            if diag:
                w = jnp.where(causal, w, 0.0)
            acc = acc + _dot(w.astype(BF16), v)
            c = c + jnp.sum(spm, axis=1, keepdims=True)
            new.append((c, acc))
        return tuple(new)

    init = tuple((jnp.zeros((t, 1), F32), jnp.zeros((t, LANES), F32)) for _ in range(2))
    carry = block(i, init, True)
    carry = lax.fori_loop(0, i, lambda n, cr: block(i - 1 - n, cr, False), carry)
    o_ref[...] = jnp.where(lane < SB_HEAD_DIM, carry[0][1], carry[1][1]).astype(BF16)


def _stick_breaking(qkv, bsz, seq):
    t = SB_TILE
    nq = seq // t
    T = bsz * seq
    npair = SB_HEADS // 2
    return pl.pallas_call(
        _sb_kernel,
        grid=(bsz, npair, nq),
        in_specs=[
            pl.BlockSpec((1, 1, t, LANES), lambda b, c, i: (0, c, b * nq + i, 0)),
            pl.BlockSpec((1, 1, seq, LANES), lambda b, c, i: (1, c, b, 0)),
            pl.BlockSpec((1, 1, seq, LANES), lambda b, c, i: (2, c, b, 0)),
        ],
        out_specs=pl.BlockSpec((t, LANES), lambda b, c, i: (b * nq + i, c)),
        out_shape=jax.ShapeDtypeStruct((T, SB_HEADS * SB_HEAD_DIM), BF16),
        compiler_params=_cparams("parallel", "parallel", "arbitrary"),
        name="stick_breaking",
    )(qkv, qkv, qkv)


_CONV_HALO = 8


def _conv_kernel(u_ref, w_ref, b_ref, o_ref, ext_ref):
    ts = u_ref.shape[0]
    s = pl.program_id(1)

    @pl.when(s == 0)
    def _():
        ext_ref[0:_CONV_HALO, :] = jnp.zeros((_CONV_HALO, SSD_CONV_CH), F32)

    @pl.when(s > 0)
    def _():
        ext_ref[0:_CONV_HALO, :] = ext_ref[ts:ts + _CONV_HALO, :]

    ext_ref[_CONV_HALO:_CONV_HALO + ts, :] = u_ref[...]
    cw = 512
    for c0 in range(0, SSD_CONV_CH, cw):
        acc = jnp.broadcast_to(b_ref[:, c0:c0 + cw], (ts, cw))
        for k in range(SSD_CONV):
            off = _CONV_HALO - (SSD_CONV - 1) + k
            acc = acc + w_ref[k:k + 1, c0:c0 + cw] * ext_ref[off:off + ts, c0:c0 + cw]
        o_ref[:, c0:c0 + cw] = acc * jax.nn.sigmoid(acc)


def _conv_silu(xbc, conv_w, conv_b, bsz, seq, ts=256):
    T = bsz * seq
    ns = seq // ts
    return pl.pallas_call(
        _conv_kernel,
        grid=(bsz, ns),
        in_specs=[
            pl.BlockSpec((ts, SSD_CONV_CH), lambda b, s: (b * ns + s, 0)),
            pl.BlockSpec((SSD_CONV, SSD_CONV_CH), lambda b, s: (0, 0)),
            pl.BlockSpec((1, SSD_CONV_CH), lambda b, s: (0, 0)),
        ],
        out_specs=pl.BlockSpec((ts, SSD_CONV_CH), lambda b, s: (b * ns + s, 0)),
        out_shape=jax.ShapeDtypeStruct((T, SSD_CONV_CH), F32),
        scratch_shapes=[pltpu.VMEM((ts + _CONV_HALO, SSD_CONV_CH), F32)],
        compiler_params=_cparams("parallel", "arbitrary"),
        name="conv_silu",
    )(xbc, conv_w, conv_b)


def _ssd_kernel(xs_ref, bm_ref, cm_ref, z_ref, dtr_ref, dtb_ref, a128_ref, aexp_ref, dexp_ref, ng_ref, e_ref,
                o_ref, state_ref, y_ref):
    L = SSD_CHUNK
    c = pl.program_id(1)

    @pl.when(c == 0)
    def _():
        state_ref[...] = jnp.zeros_like(state_ref)

    row = lax.broadcasted_iota(jnp.int32, (L, L), 0)
    col = lax.broadcasted_iota(jnp.int32, (L, L), 1)
    incl = (col <= row).astype(F32)
    eye = (col == row).astype(F32)
    lower = col <= row
    lane = lax.broadcasted_iota(jnp.int32, (L, LANES), 1)

    dt = jax.nn.softplus(dtr_ref[...] + dtb_ref[...])
    dt_exp = _dot(dt, e_ref[...], HIGHEST)
    cs_exp = _dot(incl, dt_exp * aexp_ref[...], HIGHEST)
    cs_col = _dot(incl, dt * a128_ref[...], HIGHEST)
    cs_row = _dot_nt(eye, cs_col, HIGHEST)

    xs = xs_ref[...]
    xdt = xs * dt_exp
    xdt_b = xdt.astype(BF16)
    cs_last = cs_exp[L - 1:L, :]
    xw_b = (xdt * jnp.exp(cs_last - cs_exp)).astype(BF16)
    grow = jnp.exp(cs_exp)
    grow_last = jnp.exp(cs_last)

    gw = SSD_GROUP_WIDTH
    heads_per_group = SSD_HEADS // SSD_GROUPS
    for g in range(SSD_GROUPS):
        bg = bm_ref[:, g * SSD_STATE:(g + 1) * SSD_STATE]
        cg = cm_ref[:, g * SSD_STATE:(g + 1) * SSD_STATE].astype(BF16)
        cb = _dot_nt(cg, bg.astype(BF16))
        for pair in range(heads_per_group // 2):
            c0 = g * gw + pair * LANES
            x_pair = xdt_b[:, c0:c0 + LANES]
            ys = []
            for sub in range(2):
                hd = g * heads_per_group + pair * 2 + sub
                diff = cs_col[:, hd:hd + 1] - cs_row[hd:hd + 1, :]
                seg = jnp.where(lower, jnp.exp(jnp.minimum(diff, 0.0)), 0.0)
                ys.append(_dot((cb * seg).astype(BF16), x_pair))
            y_ref[:, c0:c0 + LANES] = jnp.where(lane < SSD_HEAD_DIM, ys[0], ys[1])
        st = state_ref[g]
        y_ref[:, g * gw:(g + 1) * gw] += _dot(cg, st.astype(BF16)) * grow[:, g * gw:(g + 1) * gw]
        state_ref[g] = st * grow_last[:, g * gw:(g + 1) * gw] + _dot(bg.T.astype(BF16), xw_b[:, g * gw:(g + 1) * gw])

    z = z_ref[...]
    y = (y_ref[...] + dexp_ref[...] * xs) * (z * jax.nn.sigmoid(z))
    for g in range(SSD_GROUPS):
        yg = y[:, g * gw:(g + 1) * gw]
        ms = jnp.mean(yg * yg, axis=-1, keepdims=True)
        o_ref[:, g * gw:(g + 1) * gw] = (yg * lax.rsqrt(ms + LN_EPS) * ng_ref[:, g * gw:(g + 1) * gw]).astype(BF16)


def _ssd(xbc_act, z, dt_raw, dt_bias128, a128, a_exp, d_exp, norm_g, expand, bsz, seq):
    L = SSD_CHUNK
    T = bsz * seq
    nc = seq // L
    nb = SSD_INNER // SSD_GROUP_WIDTH
    full = lambda shape: pl.BlockSpec(shape, lambda b, c: (0,) * len(shape))
    return pl.pallas_call(
        _ssd_kernel,
        grid=(bsz, nc),
        in_specs=[
            pl.BlockSpec((L, SSD_INNER), lambda b, c: (b * nc + c, 0)),
            pl.BlockSpec((L, SSD_GROUPS * SSD_STATE), lambda b, c: (b * nc + c, nb)),
            pl.BlockSpec((L, SSD_GROUPS * SSD_STATE), lambda b, c: (b * nc + c, nb + 1)),
            pl.BlockSpec((L, SSD_INNER), lambda b, c: (b * nc + c, 0)),
            pl.BlockSpec((L, LANES), lambda b, c: (b * nc + c, 0)),
            full((1, LANES)), full((1, LANES)), full((1, SSD_INNER)), full((1, SSD_INNER)), full((1, SSD_INNER)),
            full((LANES, SSD_INNER)),
        ],
        out_specs=pl.BlockSpec((L, SSD_INNER), lambda b, c: (b * nc + c, 0)),
        out_shape=jax.ShapeDtypeStruct((T, SSD_INNER), BF16),
        scratch_shapes=[
            pltpu.VMEM((SSD_GROUPS, SSD_STATE, SSD_GROUP_WIDTH), F32),
            pltpu.VMEM((L, SSD_INNER), F32),
        ],
        compiler_params=_cparams("parallel", "arbitrary"),
        name="ssd_scan",
    )(xbc_act, xbc_act, xbc_act, z, dt_raw, dt_bias128, a128, a_exp, d_exp, norm_g, expand)


def _merge_kernel(osb_ref, ossd_ref, gl_ref, h_ref, bg_ref, wsb_ref, wssd_ref, wmix_ref, g_ref, b_ref, o_ref):
    gates = jax.nn.sigmoid(gl_ref[...] + bg_ref[...])
    merged = (gates[:, :D_MODEL] * _dot(osb_ref[...], wsb_ref[...])
              + gates[:, D_MODEL:] * _dot(ossd_ref[...], wssd_ref[...]))
    mix = _dot(merged.astype(BF16), wmix_ref[...])
    o_ref[...] = _layer_norm(DN_ALPHA * h_ref[...] + mix, g_ref[...], b_ref[...])


def _merge(o_sb, o_ssd, gate_logits, h, b_gate, w_sb, w_ssd, w_mix, ln_g, ln_b, tm=256):
    T = h.shape[0]
    full = lambda shape: pl.BlockSpec(shape, lambda i: (0,) * len(shape))
    return pl.pallas_call(
        _merge_kernel,
        grid=(T // tm,),
        in_specs=[
            pl.BlockSpec((tm, D_MODEL), lambda i: (i, 0)),
            pl.BlockSpec((tm, SSD_INNER), lambda i: (i, 0)),
            pl.BlockSpec((tm, 2 * D_MODEL), lambda i: (i, 0)),
            pl.BlockSpec((tm, D_MODEL), lambda i: (i, 0)),
            full((1, 2 * D_MODEL)), full((D_MODEL, D_MODEL)), full((SSD_INNER, D_MODEL)), full((D_MODEL, D_MODEL)),
            full((1, D_MODEL)), full((1, D_MODEL)),
        ],
        out_specs=pl.BlockSpec((tm, D_MODEL), lambda i: (i, 0)),
        out_shape=jax.ShapeDtypeStruct((T, D_MODEL), F32),
        compiler_params=_cparams("parallel"),
        name="merge",
    )(o_sb, o_ssd, gate_logits, h, b_gate, w_sb, w_ssd, w_mix, ln_g, ln_b)


def _kv_kernel(m_ref, wk_ref, wv_ref, k_ref, v_ref):
    mb = m_ref[...].astype(BF16)
    k_ref[...] = _dot(mb, wk_ref[...]).astype(BF16)
    v_ref[...] = _dot(mb, wv_ref[...]).astype(BF16)


def _kv_proj(mem2, w_xk, w_xv, tm=512):
    M = mem2.shape[0]
    full = lambda shape: pl.BlockSpec(shape, lambda i: (0,) * len(shape))
    return pl.pallas_call(
        _kv_kernel,
        grid=(M // tm,),
        in_specs=[pl.BlockSpec((tm, D_MODEL), lambda i: (i, 0)), full((D_MODEL, D_MODEL)), full((D_MODEL, D_MODEL))],
        out_specs=[pl.BlockSpec((tm, D_MODEL), lambda i: (i, 0))] * 2,
        out_shape=[jax.ShapeDtypeStruct((M, D_MODEL), BF16)] * 2,
        compiler_params=_cparams("parallel"),
        name="kv_proj",
    )(mem2, w_xk, w_xv)


def _xattn_kernel(h_ref, k_ref, v_ref, wq_ref, wo_ref, g_ref, b_ref, o_ref):
    h = h_ref[...]
    q = (_dot(h.astype(BF16), wq_ref[...]) * (X_HEAD_DIM ** -0.5)).astype(BF16)
    outs = []
    for hd in range(X_HEADS):
        sl = slice(hd * X_HEAD_DIM, (hd + 1) * X_HEAD_DIM)
        s = _dot_nt(q[:, sl], k_ref[:, sl])
        e = jnp.exp(s - jnp.max(s, axis=-1, keepdims=True))
        p = e / jnp.sum(e, axis=-1, keepdims=True)
        outs.append(_dot(p.astype(BF16), v_ref[:, sl]))
    o = jnp.concatenate(outs, axis=-1).astype(BF16)
    o_ref[...] = _layer_norm(DN_ALPHA * h + _dot(o, wo_ref[...]), g_ref[...], b_ref[...])


def _xattn(h1, k, v, w_xq, w_xo, ln_g, ln_b, bsz, seq, tm=256):
    T = bsz * seq
    nt = seq // tm
    full = lambda shape: pl.BlockSpec(shape, lambda b, i: (0,) * len(shape))
    return pl.pallas_call(
        _xattn_kernel,
        grid=(bsz, nt),
        in_specs=[
            pl.BlockSpec((tm, D_MODEL), lambda b, i: (b * nt + i, 0)),
            pl.BlockSpec((MEM_LEN, D_MODEL), lambda b, i: (b, 0)),
            pl.BlockSpec((MEM_LEN, D_MODEL), lambda b, i: (b, 0)),
            full((D_MODEL, D_MODEL)), full((D_MODEL, D_MODEL)), full((1, D_MODEL)), full((1, D_MODEL)),
        ],
        out_specs=pl.BlockSpec((tm, D_MODEL), lambda b, i: (b * nt + i, 0)),
        out_shape=jax.ShapeDtypeStruct((T, D_MODEL), F32),
        compiler_params=_cparams("parallel", "parallel"),
        name="xattn",
    )(h1, k, v, w_xq, w_xo, ln_g, ln_b)


def _router_kernel(h_ref, w_ref, b_ref, idx_ref, p_ref, rank_ref, cnt_ref, run_ref):
    tm = h_ref.shape[0]
    i = pl.program_id(0)

    @pl.when(i == 0)
    def _():
        run_ref[...] = jnp.zeros_like(run_ref)

    lane = lax.broadcasted_iota(jnp.int32, (tm, LANES), 1)
    lane_f = lane.astype(F32)
    neg = jnp.float32(-jnp.inf)
    logits = _dot(h_ref[...], w_ref[...], HIGHEST) + b_ref[...]
    work = jnp.where(lane < N_EXPERTS, logits, neg)
    vals, hots = [], []
    idx_out = jnp.zeros((tm, LANES), F32)
    for k in range(TOP_K):
        m = jnp.max(work, axis=-1, keepdims=True)
        first = jnp.min(jnp.where(work == m, lane_f, float(LANES)), axis=-1, keepdims=True)
        hot = lane_f == first
        vals.append(m)
        hots.append(hot)
        idx_out = jnp.where(lane == k, first, idx_out)
        work = jnp.where(hot, neg, work)
    exps = [jnp.exp(v - vals[0]) for v in vals]
    denom = exps[0] + exps[1] + exps[2] + exps[3]
    member = jnp.zeros((tm, LANES), F32)
    p_out = jnp.zeros((tm, LANES), F32)
    for k in range(TOP_K):
        member = jnp.where(hots[k], 1.0, member)
        p_out = jnp.where(lane == k, exps[k] / denom, p_out)
    row = lax.broadcasted_iota(jnp.int32, (tm, tm), 0)
    col = lax.broadcasted_iota(jnp.int32, (tm, tm), 1)
    before = (col < row).astype(BF16)
    earlier = _dot(before, member.astype(BF16)) + run_ref[...]
    rank_out = jnp.zeros((tm, LANES), F32)
    for k in range(TOP_K):
        r = jnp.sum(jnp.where(hots[k], earlier, 0.0), axis=-1, keepdims=True)
        rank_out = jnp.where(lane == k, r, rank_out)
    run = run_ref[...] + jnp.sum(member, axis=0, keepdims=True)
    run_ref[...] = run
    cnt_ref[...] = run
    idx_ref[...] = idx_out.astype(jnp.int32)
    p_ref[...] = p_out
    rank_ref[...] = rank_out.astype(jnp.int32)


def _router(h2, w_r, b_r, tm=512):
    T = h2.shape[0]
    full = lambda shape: pl.BlockSpec(shape, lambda i: (0,) * len(shape))
    tok = pl.BlockSpec((tm, LANES), lambda i: (i, 0))
    return pl.pallas_call(
        _router_kernel,
        grid=(T // tm,),
        in_specs=[pl.BlockSpec((tm, D_MODEL), lambda i: (i, 0)), full((D_MODEL, LANES)), full((1, LANES))],
        out_specs=[tok, tok, tok, full((1, LANES))],
        out_shape=[
            jax.ShapeDtypeStruct((T, LANES), jnp.int32),
            jax.ShapeDtypeStruct((T, LANES), F32),
            jax.ShapeDtypeStruct((T, LANES), jnp.int32),
            jax.ShapeDtypeStruct((1, LANES), F32),
        ],
        scratch_shapes=[pltpu.VMEM((1, LANES), F32)],
        compiler_params=_cparams("arbitrary"),
        name="router",
    )(h2, w_r, b_r)


def _dispatch_kernel(dest_ref, h_ref, xin_ref, x_ref, sem):
    del xin_ref
    td = dest_ref.shape[0] // TOP_K
    i = pl.program_id(0)
    n = pl.num_programs(0)

    def row_copy(t, k, slot):
        d = dest_ref[t * TOP_K + k]
        return pltpu.make_async_copy(h_ref.at[pl.ds(i * td + t, 1), :], x_ref.at[pl.ds(d, 1), :], sem.at[slot])

    def wait_all(slot):
        def body(t, _):
            for k in range(TOP_K):
                pltpu.make_async_copy(h_ref.at[pl.ds(0, 1), :], x_ref.at[pl.ds(0, 1), :], sem.at[slot]).wait()
            return 0
        lax.fori_loop(0, td, body, 0)

    slot = i % 2

    def issue(t, _):
        for k in range(TOP_K):
            row_copy(t, k, slot).start()
        return 0

    lax.fori_loop(0, td, issue, 0)

    @pl.when(i > 0)
    def _():
        wait_all(1 - slot)

    @pl.when(i == n - 1)
    def _():
        wait_all(slot)


def _dispatch(dest_flat, h2, n_rows, td=256):
    T = h2.shape[0]
    x0 = jnp.zeros((n_rows, D_MODEL), F32)
    return pl.pallas_call(
        _dispatch_kernel,
        grid=(T // td,),
        in_specs=[
            pl.BlockSpec((td * TOP_K,), lambda i: (i,), memory_space=pltpu.SMEM),
            pl.BlockSpec(memory_space=pl.ANY),
            pl.BlockSpec(memory_space=pl.ANY),
        ],
        out_specs=pl.BlockSpec(memory_space=pl.ANY),
        out_shape=jax.ShapeDtypeStruct((n_rows, D_MODEL), F32),
        scratch_shapes=[pltpu.SemaphoreType.DMA((2,))],
        input_output_aliases={2: 0},
        compiler_params=_cparams("arbitrary"),
        name="moe_dispatch",
    )(dest_flat, h2, x0)


def _expert_kernel(be_ref, nused_ref, x_ref, wg_ref, bg_ref, wu_ref, bu_ref, wd_ref, bd_ref, y_ref):
    del be_ref
    blk = pl.program_id(0)

    @pl.when(blk < nused_ref[0])
    def _():
        xb = x_ref[...].astype(BF16)
        g = jnp.minimum(_dot(xb, wg_ref[0]) + bg_ref[0], SWIGLU_LIMIT)
        u = jnp.clip(_dot(xb, wu_ref[0]) + bu_ref[0], -SWIGLU_LIMIT, SWIGLU_LIMIT)
        act = (u + 1.0) * g * jax.nn.sigmoid(SWIGLU_ALPHA * g)
        y_ref[...] = _dot(act.astype(BF16), wd_ref[0]) + bd_ref[0]

    @pl.when(blk >= nused_ref[0])
    def _():
        y_ref[...] = jnp.zeros_like(y_ref)


def _experts(block_e, n_used, x_rows, w_gate, b_gate, w_up, b_up, w_down, b_down):
    n_rows = x_rows.shape[0]
    bm = EXPERT_ROWS
    wspec = pl.BlockSpec((1, D_MODEL, D_MODEL), lambda i, be, nu: (be[i], 0, 0))
    bspec = pl.BlockSpec((1, 1, D_MODEL), lambda i, be, nu: (be[i], 0, 0))
    rows = pl.BlockSpec((bm, D_MODEL), lambda i, be, nu: (i, 0))
    return pl.pallas_call(
        _expert_kernel,
        grid_spec=pltpu.PrefetchScalarGridSpec(
            num_scalar_prefetch=2,
            grid=(n_rows // bm,),
            in_specs=[rows, wspec, bspec, wspec, bspec, wspec, bspec],
            out_specs=rows,
        ),
        out_shape=jax.ShapeDtypeStruct((n_rows, D_MODEL), F32),
        compiler_params=_cparams("arbitrary"),
        name="moe_experts",
    )(block_e, n_used, x_rows, w_gate, b_gate, w_up, b_up, w_down, b_down)


def _combine_kernel(dcur_ref, dnext_ref, y_ref, p_ref, h_ref, g_ref, b_ref, o_ref, buf, sem):
    tc = h_ref.shape[0]
    i = pl.program_id(0)
    n = pl.num_programs(0)

    def issue(dref, slot):
        def body(t, _):
            for k in range(TOP_K):
                d = dref[t * TOP_K + k]
                pltpu.make_async_copy(y_ref.at[pl.ds(d, 1), :], buf.at[slot, k, pl.ds(t, 1), :], sem.at[slot]).start()
            return 0
        lax.fori_loop(0, tc, body, 0)

    slot = i % 2

    @pl.when(i == 0)
    def _():
        issue(dcur_ref, 0)

    @pl.when(i + 1 < n)
    def _():
        issue(dnext_ref, 1 - slot)

    def wait_body(t, _):
        for k in range(TOP_K):
            pltpu.make_async_copy(y_ref.at[pl.ds(0, 1), :], buf.at[slot, k, pl.ds(0, 1), :], sem.at[slot]).wait()
        return 0

    lax.fori_loop(0, tc, wait_body, 0)

    p = p_ref[...]
    y = p[:, 0:1] * buf[slot, 0]
    for k in range(1, TOP_K):
        y = y + p[:, k:k + 1] * buf[slot, k]
    o_ref[...] = _layer_norm(DN_ALPHA * h_ref[...] + y, g_ref[...], b_ref[...])


def _combine(dest_flat, y_rows, probs, h2, ln_g, ln_b, tc=128):
    T = h2.shape[0]
    nt = T // tc
    full = lambda shape: pl.BlockSpec(shape, lambda i: (0,) * len(shape))
    return pl.pallas_call(
        _combine_kernel,
        grid=(nt,),
        in_specs=[
            pl.BlockSpec((tc * TOP_K,), lambda i: (i,), memory_space=pltpu.SMEM),
            pl.BlockSpec((tc * TOP_K,), lambda i: (jnp.minimum(i + 1, nt - 1),), memory_space=pltpu.SMEM),
            pl.BlockSpec(memory_space=pl.ANY),
            pl.BlockSpec((tc, LANES), lambda i: (i, 0)),
            pl.BlockSpec((tc, D_MODEL), lambda i: (i, 0)),
            full((1, D_MODEL)), full((1, D_MODEL)),
        ],
        out_specs=pl.BlockSpec((tc, D_MODEL), lambda i: (i, 0)),
        out_shape=jax.ShapeDtypeStruct((T, D_MODEL), F32),
        scratch_shapes=[pltpu.VMEM((2, TOP_K, tc, D_MODEL), F32), pltpu.SemaphoreType.DMA((2,))],
        compiler_params=_cparams("arbitrary"),
        name="moe_combine",
    )(dest_flat, dest_flat, y_rows, probs, h2, ln_g, ln_b)


def _pad_lanes(v, fill=0.0):
    v = v.reshape(1, -1)
    return jnp.pad(v, ((0, 0), (0, LANES - v.shape[1])), constant_values=fill)


def _mixer_stage(x2, bsz, seq, ln_in_g, ln_in_b, w_in, b_branch_gate, conv_w, conv_b, dt_bias, a_log, d_skip,
                 ssd_norm_g, w_sb, w_ssd, w_mix_out, ln1_g, ln1_b):
    n_lin = 3 * D_MODEL + SSD_INNER + SSD_CONV_CH
    w_main = jnp.concatenate([w_in[:, :n_lin], w_in[:, n_lin + SSD_HEADS:]], axis=1).astype(BF16)
    w_dt = jnp.pad(w_in[:, n_lin:n_lin + SSD_HEADS], ((0, 0), (0, LANES - SSD_HEADS))).astype(BF16)
    h, qkv, z, xbc, gate_logits, dt_raw = _inproj(x2, ln_in_g.reshape(1, -1), ln_in_b.reshape(1, -1), w_main, w_dt)

    o_sb = _stick_breaking(qkv, bsz, seq)

    xbc_act = _conv_silu(xbc, conv_w, conv_b.reshape(1, -1), bsz, seq)
    a = -jnp.exp(a_log.astype(F32))
    expand = (jnp.arange(LANES)[:, None] == (jnp.arange(SSD_INNER)[None, :] // SSD_HEAD_DIM)).astype(F32)
    o_ssd = _ssd(xbc_act, z, dt_raw, _pad_lanes(dt_bias), _pad_lanes(a),
                 jnp.repeat(a, SSD_HEAD_DIM).reshape(1, -1), jnp.repeat(d_skip, SSD_HEAD_DIM).reshape(1, -1),
                 ssd_norm_g.reshape(1, -1), expand, bsz, seq)

    return _merge(o_sb, o_ssd, gate_logits, h, b_branch_gate.reshape(1, -1), w_sb.astype(BF16), w_ssd.astype(BF16),
                  w_mix_out.astype(BF16), ln1_g.reshape(1, -1), ln1_b.reshape(1, -1))


def _xattn_stage(h1, mem2, bsz, seq, w_xq, w_xk, w_xv, w_xo, ln2_g, ln2_b):
    k, v = _kv_proj(mem2, w_xk.astype(BF16), w_xv.astype(BF16))
    return _xattn(h1, k, v, w_xq.astype(BF16), w_xo.astype(BF16), ln2_g.reshape(1, -1), ln2_b.reshape(1, -1),
                  bsz, seq)


def _moe_stage(h2, w_router, b_router, w_e_gate, b_e_gate, w_e_up, b_e_up, w_e_down, b_e_down, ln3_g, ln3_b):
    T = h2.shape[0]
    bm = EXPERT_ROWS
    w_r = jnp.pad(w_router, ((0, 0), (0, LANES - N_EXPERTS)))
    idx_p, probs, rank_p, counts_p = _router(h2, w_r, _pad_lanes(b_router))
    idx = idx_p[:, :TOP_K]
    rank = rank_p[:, :TOP_K]
    counts = counts_p[0, :N_EXPERTS].astype(jnp.int32)
    padded = (counts + bm - 1) // bm * bm
    end_padded = jnp.cumsum(padded)
    start_padded = end_padded - padded
    onehot = idx[:, :, None] == jnp.arange(N_EXPERTS, dtype=jnp.int32)[None, None, :]
    dest = jnp.sum(jnp.where(onehot, start_padded[None, None, :], 0), axis=-1) + rank
    dest_flat = dest.reshape(-1).astype(jnp.int32)
    n_blocks = -(-(T * TOP_K + N_EXPERTS * (bm - 1)) // bm)
    n_rows = n_blocks * bm
    blk_start = jnp.arange(n_blocks, dtype=jnp.int32) * bm
    block_e = jnp.minimum(jnp.sum(blk_start[:, None] >= end_padded[None, :], axis=-1), N_EXPERTS - 1).astype(jnp.int32)
    n_used = (end_padded[-1:] // bm).astype(jnp.int32)

    x_rows = _dispatch(dest_flat, h2, n_rows)
    y_rows = _experts(block_e, n_used, x_rows, w_e_gate.astype(BF16), b_e_gate[:, None, :], w_e_up.astype(BF16),
                      b_e_up[:, None, :], w_e_down.astype(BF16), b_e_down[:, None, :])
    return _combine(dest_flat, y_rows, probs, h2, ln3_g.reshape(1, -1), ln3_b.reshape(1, -1))


def kernel(x, mem, ln_in_g, ln_in_b, w_in, b_branch_gate, conv_w, conv_b, dt_bias, a_log, d_skip, ssd_norm_g, w_sb,
           w_ssd, w_mix_out, ln1_g, ln1_b, w_xq, w_xk, w_xv, w_xo, ln2_g, ln2_b, w_router, b_router, w_e_gate,
           b_e_gate, w_e_up, b_e_up, w_e_down, b_e_down, ln3_g, ln3_b):
    bsz, seq, _ = x.shape
    depth = w_in.shape[0]
    x2 = x.reshape(bsz * seq, D_MODEL)
    mem2 = mem.reshape(bsz * mem.shape[1], D_MODEL)
    assert depth == 1, "the entry LayerNorm is fused into the single layer's input projection"
    l = 0
    h1 = _mixer_stage(x2, bsz, seq, ln_in_g, ln_in_b, w_in[l], b_branch_gate[l], conv_w[l], conv_b[l], dt_bias[l],
                      a_log[l], d_skip[l], ssd_norm_g[l], w_sb[l], w_ssd[l], w_mix_out[l], ln1_g[l], ln1_b[l])
    h2 = _xattn_stage(h1, mem2, bsz, seq, w_xq[l], w_xk[l], w_xv[l], w_xo[l], ln2_g[l], ln2_b[l])
    h = _moe_stage(h2, w_router[l], b_router[l], w_e_gate[l], b_e_gate[l], w_e_up[l], b_e_up[l], w_e_down[l],
                   b_e_down[l], ln3_g[l], ln3_b[l])
    return h.reshape(bsz, seq, D_MODEL)
```

```python
import functools

import jax
import jax.numpy as jnp
from jax import lax
from jax.experimental import pallas as pl
from jax.experimental.pallas import tpu as pltpu

F32 = jnp.float32
BF16 = jnp.bfloat16
HIGHEST = lax.Precision.HIGHEST

D_MODEL = 1024
LN_EPS = 1e-5
DN_ALPHA = 2.0 ** 0.25
LANES = 128

SB_HEADS = 16
SB_HEAD_DIM = 64
SB_TILE = 256
SB_LANES = 256
SB_GROUP = SB_LANES // SB_HEAD_DIM
LOG2E = 1.4426950408889634
SB_SKIP_BITS = 152.0

SSD_INNER = 2048
SSD_HEADS = 32
SSD_HEAD_DIM = 64
SSD_GROUPS = 4
SSD_GROUP_WIDTH = SSD_INNER // SSD_GROUPS
SSD_STATE = 128
SSD_CONV = 4
SSD_CONV_CH = SSD_INNER + 2 * SSD_GROUPS * SSD_STATE
SSD_CHUNK = 128

X_HEADS = 4
X_HEAD_DIM = 256
MEM_LEN = 256

N_EXPERTS = 32
TOP_K = 4
SWIGLU_LIMIT = 7.0
SWIGLU_ALPHA = 1.702
EXPERT_ROWS = 256

VMEM_LIMIT = 52 * 1024 * 1024


def _cparams(*sem):
    return pltpu.CompilerParams(dimension_semantics=sem, vmem_limit_bytes=VMEM_LIMIT)


def _layer_norm(x, g, b):
    mu = jnp.mean(x, axis=-1, keepdims=True)
    xc = x - mu
    var = jnp.mean(xc * xc, axis=-1, keepdims=True)
    return xc * lax.rsqrt(var + LN_EPS) * g + b


def _dot(a, b, precision=None):
    return jnp.dot(a, b, preferred_element_type=F32, precision=precision)


def _dot_nt(a, b, precision=None):
    return lax.dot_general(a, b, (((1,), (1,)), ((), ())), preferred_element_type=F32, precision=precision)


_N_QKV, _N_Z, _N_XBC, _N_GATE = 3, 2, 3, 2


def _inproj_kernel(x_ref, g_ref, b_ref, w_ref, wdt_ref, h_ref, qkv_ref, z_ref, xbc_ref, gate_ref, dt_ref, hb_ref):
    j = pl.program_id(1)

    @pl.when(j == 0)
    def _():
        h = _layer_norm(x_ref[...], g_ref[...], b_ref[...])
        h_ref[...] = h
        hb = h.astype(BF16)
        hb_ref[...] = hb
        dt_ref[...] = _dot(hb, wdt_ref[...])

    @pl.when(j < _N_QKV)
    def _():
        for c in range(D_MODEL // SB_LANES):
            sl = slice(c * SB_LANES, (c + 1) * SB_LANES)
            qkv_ref[0, c] = _dot(hb_ref[...], w_ref[:, sl]).astype(BF16)

    @pl.when((j >= _N_QKV) & (j < _N_QKV + _N_Z))
    def _():
        z_ref[...] = _dot(hb_ref[...], w_ref[...])

    @pl.when((j >= _N_QKV + _N_Z) & (j < _N_QKV + _N_Z + _N_XBC))
    def _():
        xbc_ref[...] = _dot(hb_ref[...], w_ref[...])

    @pl.when(j >= _N_QKV + _N_Z + _N_XBC)
    def _():
        gate_ref[...] = _dot(hb_ref[...], w_ref[...])


def _inproj(x2, ln_g, ln_b, w_main, w_dt, tm=512):
    T = x2.shape[0]
    nj = _N_QKV + _N_Z + _N_XBC + _N_GATE
    o_z, o_xbc, o_gate = _N_QKV, _N_QKV + _N_Z, _N_QKV + _N_Z + _N_XBC
    return pl.pallas_call(
        _inproj_kernel,
        grid=(T // tm, nj),
        in_specs=[
            pl.BlockSpec((tm, D_MODEL), lambda i, j: (i, 0)),
            pl.BlockSpec((1, D_MODEL), lambda i, j: (0, 0)),
            pl.BlockSpec((1, D_MODEL), lambda i, j: (0, 0)),
            pl.BlockSpec((D_MODEL, D_MODEL), lambda i, j: (0, j)),
            pl.BlockSpec((D_MODEL, LANES), lambda i, j: (0, 0)),
        ],
        out_specs=[
            pl.BlockSpec((tm, D_MODEL), lambda i, j: (i, 0)),
            pl.BlockSpec((1, D_MODEL // SB_LANES, tm, SB_LANES), lambda i, j: (jnp.minimum(j, _N_QKV - 1), 0, i, 0)),
            pl.BlockSpec((tm, D_MODEL), lambda i, j: (i, jnp.clip(j - o_z, 0, _N_Z - 1))),
            pl.BlockSpec((tm, D_MODEL), lambda i, j: (i, jnp.clip(j - o_xbc, 0, _N_XBC - 1))),
            pl.BlockSpec((tm, D_MODEL), lambda i, j: (i, jnp.clip(j - o_gate, 0, _N_GATE - 1))),
            pl.BlockSpec((tm, LANES), lambda i, j: (i, 0)),
        ],
        out_shape=[
            jax.ShapeDtypeStruct((T, D_MODEL), F32),
            jax.ShapeDtypeStruct((3, D_MODEL // SB_LANES, T, SB_LANES), BF16),
            jax.ShapeDtypeStruct((T, _N_Z * D_MODEL), F32),
            jax.ShapeDtypeStruct((T, _N_XBC * D_MODEL), F32),
            jax.ShapeDtypeStruct((T, _N_GATE * D_MODEL), F32),
            jax.ShapeDtypeStruct((T, LANES), F32),
        ],
        scratch_shapes=[pltpu.VMEM((tm, D_MODEL), BF16)],
        compiler_params=_cparams("parallel", "arbitrary"),
        name="inproj",
    )(x2, ln_g, ln_b, w_main, w_dt)


def _sb_kernel(q_ref, k_ref, v_ref, o_ref):
    t = SB_TILE
    G = SB_GROUP
    i = pl.program_id(2)
    lane = lax.broadcasted_iota(jnp.int32, (t, SB_LANES), 1)
    head_lanes = [(lane >= h * SB_HEAD_DIM) & (lane < (h + 1) * SB_HEAD_DIM) for h in range(G)]
    qs = q_ref[0, 0] * BF16(SB_HEAD_DIM ** -0.5)
    zero = jnp.zeros_like(qs)
    q_stack = jnp.concatenate([jnp.where(m, qs, zero) for m in head_lanes], axis=0)
    row = lax.broadcasted_iota(jnp.int32, (G * t, t), 0) & (t - 1)
    col = lax.broadcasted_iota(jnp.int32, (G * t, t), 1)
    causal = col < row
    krow = lax.broadcasted_iota(jnp.int32, (2 * t, t), 0) & (t - 1)
    kcol = lax.broadcasted_iota(jnp.int32, (2 * t, t), 1)
    later2 = (krow > kcol).astype(BF16)

    def block(jb, c, acc, diag):
        k = k_ref[0, 0, pl.ds(pl.multiple_of(jb * t, t), t), :]
        v = v_ref[0, 0, pl.ds(pl.multiple_of(jb * t, t), t), :]
        vzero = jnp.zeros_like(v)
        v_stack = jnp.concatenate([jnp.where(m, v, vzero) for m in head_lanes], axis=0)
        z2 = _dot_nt(q_stack, k) * LOG2E
        neg_abs = lax.bitcast_convert_type(lax.bitcast_convert_type(z2, jnp.uint32) | jnp.uint32(0x80000000), F32)
        sp2 = jnp.maximum(z2, 0.0) + jnp.log2(1.0 + jnp.exp2(neg_abs))
        spm = jnp.where(causal, sp2, 0.0) if diag else sp2
        hi = spm.astype(BF16)
        lo = (spm - hi.astype(F32)).astype(BF16)
        after = _dot(jnp.concatenate([hi, lo], axis=1), later2)
        w = jnp.exp2(z2 - sp2 - after - c)
        if diag:
            w = jnp.where(causal, w, 0.0)
        wb = w.astype(BF16)
        w_cat = jnp.concatenate([wb[h * t:(h + 1) * t] for h in range(G)], axis=1)
        return c + jnp.sum(spm, axis=1, keepdims=True), acc + _dot(w_cat, v_stack)

    c, acc = block(i, jnp.zeros((G * t, 1), F32), jnp.zeros((t, SB_LANES), F32), True)

    def cond(carry):
        jb, cmin, _, _ = carry
        return (jb >= 0) & (cmin < SB_SKIP_BITS)

    def body(carry):
        jb, _, c, acc = carry
        c, acc = block(jb, c, acc, False)
        return jb - 1, jnp.min(c), c, acc

    _, _, _, acc = lax.while_loop(cond, body, (i - 1, jnp.min(c), c, acc))
    o_ref[...] = acc.astype(BF16)


def _stick_breaking(qkv, bsz, seq):
    t = SB_TILE
    nq = seq // t
    T = bsz * seq
    return pl.pallas_call(
        _sb_kernel,
        grid=(bsz, SB_HEADS // SB_GROUP, nq),
        in_specs=[
            pl.BlockSpec((1, 1, t, SB_LANES), lambda b, c, i: (0, c, b * nq + i, 0)),
            pl.BlockSpec((1, 1, seq, SB_LANES), lambda b, c, i: (1, c, b, 0)),
            pl.BlockSpec((1, 1, seq, SB_LANES), lambda b, c, i: (2, c, b, 0)),
        ],
        out_specs=pl.BlockSpec((t, SB_LANES), lambda b, c, i: (b * nq + i, c)),
        out_shape=jax.ShapeDtypeStruct((T, SB_HEADS * SB_HEAD_DIM), BF16),
        compiler_params=_cparams("parallel", "parallel", "arbitrary"),
        name="stick_breaking",
    )(qkv, qkv, qkv)


_CONV_HALO = 8


def _conv_kernel(u_ref, w_ref, b_ref, o_ref, ext_ref):
    ts = u_ref.shape[0]
    s = pl.program_id(1)

    @pl.when(s == 0)
    def _():
        ext_ref[0:_CONV_HALO, :] = jnp.zeros((_CONV_HALO, SSD_CONV_CH), F32)

    @pl.when(s > 0)
    def _():
        ext_ref[0:_CONV_HALO, :] = ext_ref[ts:ts + _CONV_HALO, :]

    ext_ref[_CONV_HALO:_CONV_HALO + ts, :] = u_ref[...]
    cw = 512
    for c0 in range(0, SSD_CONV_CH, cw):
        acc = jnp.broadcast_to(b_ref[:, c0:c0 + cw], (ts, cw))
        for k in range(SSD_CONV):
            off = _CONV_HALO - (SSD_CONV - 1) + k
            acc = acc + w_ref[k:k + 1, c0:c0 + cw] * ext_ref[off:off + ts, c0:c0 + cw]
        o_ref[:, c0:c0 + cw] = acc * jax.nn.sigmoid(acc)


def _conv_silu(xbc, conv_w, conv_b, bsz, seq, ts=256):
    T = bsz * seq
    ns = seq // ts
    return pl.pallas_call(
        _conv_kernel,
        grid=(bsz, ns),
        in_specs=[
            pl.BlockSpec((ts, SSD_CONV_CH), lambda b, s: (b * ns + s, 0)),
            pl.BlockSpec((SSD_CONV, SSD_CONV_CH), lambda b, s: (0, 0)),
            pl.BlockSpec((1, SSD_CONV_CH), lambda b, s: (0, 0)),
        ],
        out_specs=pl.BlockSpec((ts, SSD_CONV_CH), lambda b, s: (b * ns + s, 0)),
        out_shape=jax.ShapeDtypeStruct((T, SSD_CONV_CH), F32),
        scratch_shapes=[pltpu.VMEM((ts + _CONV_HALO, SSD_CONV_CH), F32)],
        compiler_params=_cparams("parallel", "arbitrary"),
        name="conv_silu",
    )(xbc, conv_w, conv_b)


def _ssd_kernel(xs_ref, bm_ref, cm_ref, z_ref, dtr_ref, dtb_ref, a128_ref, dexp_ref, ng_ref, e_ref,
                o_ref, state_ref, y_ref):
    L = SSD_CHUNK
    c = pl.program_id(1)

    @pl.when(c == 0)
    def _():
        state_ref[...] = jnp.zeros_like(state_ref)

    row = lax.broadcasted_iota(jnp.int32, (L, L), 0)
    col = lax.broadcasted_iota(jnp.int32, (L, L), 1)
    incl = (col <= row).astype(F32)
    eye = (col == row).astype(F32)
    lower = col <= row
    lane = lax.broadcasted_iota(jnp.int32, (L, LANES), 1)

    def expand(v):
        hi = v.astype(BF16)
        r1 = v - hi.astype(F32)
        mid = r1.astype(BF16)
        lo = (r1 - mid.astype(F32)).astype(BF16)
        return _dot(jnp.concatenate([hi, mid, lo], axis=1), e_ref[...])

    dt = jax.nn.softplus(dtr_ref[...] + dtb_ref[...])
    dt_exp = expand(dt)
    cs_col = _dot(incl, dt * a128_ref[...], HIGHEST)
    cs_exp = expand(cs_col)
    cs_row = _dot_nt(eye, cs_col, HIGHEST)

    xs = xs_ref[...]
    xdt = xs * dt_exp
    xdt_b = xdt.astype(BF16)
    cs_last = cs_exp[L - 1:L, :]
    xw_b = (xdt * jnp.exp(cs_last - cs_exp)).astype(BF16)
    grow = jnp.exp(cs_exp)
    grow_last = jnp.exp(cs_last)

    gw = SSD_GROUP_WIDTH
    heads_per_group = SSD_HEADS // SSD_GROUPS
    for g in range(SSD_GROUPS):
        bg = bm_ref[:, g * SSD_STATE:(g + 1) * SSD_STATE]
        cg = cm_ref[:, g * SSD_STATE:(g + 1) * SSD_STATE].astype(BF16)
        cb = _dot_nt(cg, bg.astype(BF16))
        for pair in range(heads_per_group // 2):
            c0 = g * gw + pair * LANES
            x_pair = xdt_b[:, c0:c0 + LANES]
            ys = []
            for sub in range(2):
                hd = g * heads_per_group + pair * 2 + sub
                diff = cs_col[:, hd:hd + 1] - cs_row[hd:hd + 1, :]
                seg = jnp.where(lower, jnp.exp(jnp.minimum(diff, 0.0)), 0.0)
                ys.append(_dot((cb * seg).astype(BF16), x_pair))
            y_ref[:, c0:c0 + LANES] = jnp.where(lane < SSD_HEAD_DIM, ys[0], ys[1])
        st = state_ref[g]
        y_ref[:, g * gw:(g + 1) * gw] += _dot(cg, st.astype(BF16)) * grow[:, g * gw:(g + 1) * gw]
        state_ref[g] = st * grow_last[:, g * gw:(g + 1) * gw] + _dot(bg.T.astype(BF16), xw_b[:, g * gw:(g + 1) * gw])

    z = z_ref[...]
    y = (y_ref[...] + dexp_ref[...] * xs) * (z * jax.nn.sigmoid(z))
    for g in range(SSD_GROUPS):
        yg = y[:, g * gw:(g + 1) * gw]
        ms = jnp.mean(yg * yg, axis=-1, keepdims=True)
        o_ref[:, g * gw:(g + 1) * gw] = (yg * lax.rsqrt(ms + LN_EPS) * ng_ref[:, g * gw:(g + 1) * gw]).astype(BF16)


def _ssd(xbc_act, z, dt_raw, dt_bias128, a128, d_exp, norm_g, bsz, seq):
    L = SSD_CHUNK
    T = bsz * seq
    nc = seq // L
    nb = SSD_INNER // SSD_GROUP_WIDTH
    head_of_row = jnp.arange(3 * LANES) % LANES
    expand = (head_of_row[:, None] == (jnp.arange(SSD_INNER)[None, :] // SSD_HEAD_DIM)).astype(BF16)
    full = lambda shape: pl.BlockSpec(shape, lambda b, c: (0,) * len(shape))
    return pl.pallas_call(
        _ssd_kernel,
        grid=(bsz, nc),
        in_specs=[
            pl.BlockSpec((L, SSD_INNER), lambda b, c: (b * nc + c, 0)),
            pl.BlockSpec((L, SSD_GROUPS * SSD_STATE), lambda b, c: (b * nc + c, nb)),
            pl.BlockSpec((L, SSD_GROUPS * SSD_STATE), lambda b, c: (b * nc + c, nb + 1)),
            pl.BlockSpec((L, SSD_INNER), lambda b, c: (b * nc + c, 0)),
            pl.BlockSpec((L, LANES), lambda b, c: (b * nc + c, 0)),
            full((1, LANES)), full((1, LANES)), full((1, SSD_INNER)), full((1, SSD_INNER)),
            full((3 * LANES, SSD_INNER)),
        ],
        out_specs=pl.BlockSpec((L, SSD_INNER), lambda b, c: (b * nc + c, 0)),
        out_shape=jax.ShapeDtypeStruct((T, SSD_INNER), BF16),
        scratch_shapes=[
            pltpu.VMEM((SSD_GROUPS, SSD_STATE, SSD_GROUP_WIDTH), F32),
            pltpu.VMEM((L, SSD_INNER), F32),
        ],
        compiler_params=_cparams("parallel", "arbitrary"),
        name="ssd_scan",
    )(xbc_act, xbc_act, xbc_act, z, dt_raw, dt_bias128, a128, d_exp, norm_g, expand)


def _merge_kernel(osb_ref, ossd_ref, gl_ref, h_ref, bg_ref, wsb_ref, wssd_ref, wmix_ref, g_ref, b_ref, o_ref):
    gates = jax.nn.sigmoid(gl_ref[...] + bg_ref[...])
    merged = (gates[:, :D_MODEL] * _dot(osb_ref[...], wsb_ref[...])
              + gates[:, D_MODEL:] * _dot(ossd_ref[...], wssd_ref[...]))
    mix = _dot(merged.astype(BF16), wmix_ref[...])
    o_ref[...] = _layer_norm(DN_ALPHA * h_ref[...] + mix, g_ref[...], b_ref[...])


def _merge(o_sb, o_ssd, gate_logits, h, b_gate, w_sb, w_ssd, w_mix, ln_g, ln_b, tm=256):
    T = h.shape[0]
    full = lambda shape: pl.BlockSpec(shape, lambda i: (0,) * len(shape))
    return pl.pallas_call(
        _merge_kernel,
        grid=(T // tm,),
        in_specs=[
            pl.BlockSpec((tm, D_MODEL), lambda i: (i, 0)),
            pl.BlockSpec((tm, SSD_INNER), lambda i: (i, 0)),
            pl.BlockSpec((tm, 2 * D_MODEL), lambda i: (i, 0)),
            pl.BlockSpec((tm, D_MODEL), lambda i: (i, 0)),
            full((1, 2 * D_MODEL)), full((D_MODEL, D_MODEL)), full((SSD_INNER, D_MODEL)), full((D_MODEL, D_MODEL)),
            full((1, D_MODEL)), full((1, D_MODEL)),
        ],
        out_specs=pl.BlockSpec((tm, D_MODEL), lambda i: (i, 0)),
        out_shape=jax.ShapeDtypeStruct((T, D_MODEL), F32),
        compiler_params=_cparams("parallel"),
        name="merge",
    )(o_sb, o_ssd, gate_logits, h, b_gate, w_sb, w_ssd, w_mix, ln_g, ln_b)


def _kv_kernel(m_ref, wk_ref, wv_ref, k_ref, v_ref):
    mb = m_ref[...].astype(BF16)
    k_ref[...] = _dot(mb, wk_ref[...]).astype(BF16)
    v_ref[...] = _dot(mb, wv_ref[...]).astype(BF16)


def _kv_proj(mem2, w_xk, w_xv, tm=512):
    M = mem2.shape[0]
    full = lambda shape: pl.BlockSpec(shape, lambda i: (0,) * len(shape))
    return pl.pallas_call(
        _kv_kernel,
        grid=(M // tm,),
        in_specs=[pl.BlockSpec((tm, D_MODEL), lambda i: (i, 0)), full((D_MODEL, D_MODEL)), full((D_MODEL, D_MODEL))],
        out_specs=[pl.BlockSpec((tm, D_MODEL), lambda i: (i, 0))] * 2,
        out_shape=[jax.ShapeDtypeStruct((M, D_MODEL), BF16)] * 2,
        compiler_params=_cparams("parallel"),
        name="kv_proj",
    )(mem2, w_xk, w_xv)


def _pack_bf16_pairs(x):
    bits = lax.bitcast_convert_type(x.astype(BF16).astype(F32), jnp.uint32)
    half = x.shape[1] // 2
    return (bits[:, :half] >> 16) | (bits[:, half:] & jnp.uint32(0xFFFF0000))


def _unpack_bf16_pairs(u):
    lo = lax.bitcast_convert_type(u << 16, F32)
    hi = lax.bitcast_convert_type(u & jnp.uint32(0xFFFF0000), F32)
    return jnp.concatenate([lo, hi], axis=1).astype(BF16)


def _xattn_kernel(h_ref, k_ref, v_ref, wq_ref, wo_ref, g_ref, b_ref, o_ref, op_ref):
    h = h_ref[...]
    q = (_dot(h.astype(BF16), wq_ref[...]) * (X_HEAD_DIM ** -0.5)).astype(BF16)
    outs = []
    for hd in range(X_HEADS):
        sl = slice(hd * X_HEAD_DIM, (hd + 1) * X_HEAD_DIM)
        s = _dot_nt(q[:, sl], k_ref[:, sl])
        e = jnp.exp(s - jnp.max(s, axis=-1, keepdims=True))
        p = e / jnp.sum(e, axis=-1, keepdims=True)
        outs.append(_dot(p.astype(BF16), v_ref[:, sl]))
    o = jnp.concatenate(outs, axis=-1).astype(BF16)
    h2 = _layer_norm(DN_ALPHA * h + _dot(o, wo_ref[...]), g_ref[...], b_ref[...])
    o_ref[...] = h2
    op_ref[...] = _pack_bf16_pairs(h2)


def _xattn(h1, k, v, w_xq, w_xo, ln_g, ln_b, bsz, seq, tm=256):
    T = bsz * seq
    nt = seq // tm
    full = lambda shape: pl.BlockSpec(shape, lambda b, i: (0,) * len(shape))
    return pl.pallas_call(
        _xattn_kernel,
        grid=(bsz, nt),
        in_specs=[
            pl.BlockSpec((tm, D_MODEL), lambda b, i: (b * nt + i, 0)),
            pl.BlockSpec((MEM_LEN, D_MODEL), lambda b, i: (b, 0)),
            pl.BlockSpec((MEM_LEN, D_MODEL), lambda b, i: (b, 0)),
            full((D_MODEL, D_MODEL)), full((D_MODEL, D_MODEL)), full((1, D_MODEL)), full((1, D_MODEL)),
        ],
        out_specs=[pl.BlockSpec((tm, D_MODEL), lambda b, i: (b * nt + i, 0)),
                   pl.BlockSpec((tm, D_MODEL // 2), lambda b, i: (b * nt + i, 0))],
        out_shape=[jax.ShapeDtypeStruct((T, D_MODEL), F32), jax.ShapeDtypeStruct((T, D_MODEL // 2), jnp.uint32)],
        compiler_params=_cparams("parallel", "parallel"),
        name="xattn",
    )(h1, k, v, w_xq, w_xo, ln_g, ln_b)


def _router_kernel(h_ref, w_ref, b_ref, idx_ref, p_ref, rank_ref, cnt_ref, run_ref):
    tm = h_ref.shape[0]
    i = pl.program_id(0)

    @pl.when(i == 0)
    def _():
        run_ref[...] = jnp.zeros_like(run_ref)

    lane = lax.broadcasted_iota(jnp.int32, (tm, LANES), 1)
    lane_f = lane.astype(F32)
    neg = jnp.float32(-jnp.inf)
    logits = _dot(h_ref[...], w_ref[...], HIGHEST) + b_ref[...]
    work = jnp.where(lane < N_EXPERTS, logits, neg)
    vals, hots = [], []
    idx_out = jnp.zeros((tm, LANES), F32)
    for k in range(TOP_K):
        m = jnp.max(work, axis=-1, keepdims=True)
        first = jnp.min(jnp.where(work == m, lane_f, float(LANES)), axis=-1, keepdims=True)
        hot = lane_f == first
        vals.append(m)
        hots.append(hot)
        idx_out = jnp.where(lane == k, first, idx_out)
        work = jnp.where(hot, neg, work)
    exps = [jnp.exp(v - vals[0]) for v in vals]
    denom = exps[0] + exps[1] + exps[2] + exps[3]
    member = jnp.zeros((tm, LANES), F32)
    p_out = jnp.zeros((tm, LANES), F32)
    for k in range(TOP_K):
        member = jnp.where(hots[k], 1.0, member)
        p_out = jnp.where(lane == k, exps[k] / denom, p_out)
    row = lax.broadcasted_iota(jnp.int32, (tm, tm), 0)
    col = lax.broadcasted_iota(jnp.int32, (tm, tm), 1)
    before = (col < row).astype(BF16)
    earlier = _dot(before, member.astype(BF16)) + run_ref[...]
    rank_out = jnp.zeros((tm, LANES), F32)
    for k in range(TOP_K):
        r = jnp.sum(jnp.where(hots[k], earlier, 0.0), axis=-1, keepdims=True)
        rank_out = jnp.where(lane == k, r, rank_out)
    run = run_ref[...] + jnp.sum(member, axis=0, keepdims=True)
    run_ref[...] = run
    cnt_ref[...] = run
    idx_ref[...] = idx_out.astype(jnp.int32)
    p_ref[...] = p_out
    rank_ref[...] = rank_out.astype(jnp.int32)


def _router(h2, w_r, b_r, tm=512):
    T = h2.shape[0]
    full = lambda shape: pl.BlockSpec(shape, lambda i: (0,) * len(shape))
    tok = pl.BlockSpec((tm, LANES), lambda i: (i, 0))
    return pl.pallas_call(
        _router_kernel,
        grid=(T // tm,),
        in_specs=[pl.BlockSpec((tm, D_MODEL), lambda i: (i, 0)), full((D_MODEL, LANES)), full((1, LANES))],
        out_specs=[tok, tok, tok, full((1, LANES))],
        out_shape=[
            jax.ShapeDtypeStruct((T, LANES), jnp.int32),
            jax.ShapeDtypeStruct((T, LANES), F32),
            jax.ShapeDtypeStruct((T, LANES), jnp.int32),
            jax.ShapeDtypeStruct((1, LANES), F32),
        ],
        scratch_shapes=[pltpu.VMEM((1, LANES), F32)],
        compiler_params=_cparams("arbitrary"),
        name="router",
    )(h2, w_r, b_r)


def _dispatch_kernel(dest_ref, h_ref, xin_ref, x_ref, sem):
    del xin_ref
    td = h_ref.shape[0]

    def issue(t, _):
        for k in range(TOP_K):
            d = dest_ref[t * TOP_K + k]
            pltpu.make_async_copy(h_ref.at[pl.ds(t, 1), :], x_ref.at[pl.ds(d, 1), :], sem).start()
        return 0

    lax.fori_loop(0, td, issue, 0)

    def wait(t, _):
        for k in range(TOP_K):
            pltpu.make_async_copy(h_ref.at[pl.ds(0, 1), :], x_ref.at[pl.ds(0, 1), :], sem).wait()
        return 0

    lax.fori_loop(0, td, wait, 0)


def _dispatch(dest_flat, h2_packed, n_rows, td=256):
    T, width = h2_packed.shape
    x0 = jnp.zeros((n_rows, width), h2_packed.dtype)
    return pl.pallas_call(
        _dispatch_kernel,
        grid=(T // td,),
        in_specs=[
            pl.BlockSpec((td * TOP_K,), lambda i: (i,), memory_space=pltpu.SMEM),
            pl.BlockSpec((td, width), lambda i: (i, 0)),
            pl.BlockSpec(memory_space=pl.ANY),
        ],
        out_specs=pl.BlockSpec(memory_space=pl.ANY),
        out_shape=jax.ShapeDtypeStruct((n_rows, width), h2_packed.dtype),
        scratch_shapes=[pltpu.SemaphoreType.DMA(())],
        input_output_aliases={2: 0},
        compiler_params=_cparams("arbitrary"),
        name="moe_dispatch",
    )(dest_flat, h2_packed, x0)


def _expert_kernel(be_ref, nused_ref, x_ref, wg_ref, bg_ref, wu_ref, bu_ref, wd_ref, bd_ref, y_ref,
                   wg_bf, wu_bf, wd_bf):
    blk = pl.program_id(0)
    new_expert = (blk == 0) | (be_ref[blk] != be_ref[jnp.maximum(blk - 1, 0)])

    @pl.when(new_expert & (blk < nused_ref[0]))
    def _():
        wg_bf[...] = wg_ref[0].astype(BF16)
        wu_bf[...] = wu_ref[0].astype(BF16)
        wd_bf[...] = wd_ref[0].astype(BF16)

    @pl.when(blk < nused_ref[0])
    def _():
        xb = _unpack_bf16_pairs(x_ref[...])
        g = jnp.minimum(_dot(xb, wg_bf[...]) + bg_ref[0], SWIGLU_LIMIT)
        u = jnp.clip(_dot(xb, wu_bf[...]) + bu_ref[0], -SWIGLU_LIMIT, SWIGLU_LIMIT)
        act = (u + 1.0) * g * jax.nn.sigmoid(SWIGLU_ALPHA * g)
        y_ref[...] = _dot(act.astype(BF16), wd_bf[...]) + bd_ref[0]

    @pl.when(blk >= nused_ref[0])
    def _():
        y_ref[...] = jnp.zeros_like(y_ref)


def _experts(block_e, n_used, x_rows, w_gate, b_gate, w_up, b_up, w_down, b_down):
    n_rows = x_rows.shape[0]
    bm = EXPERT_ROWS
    wspec = pl.BlockSpec((1, D_MODEL, D_MODEL), lambda i, be, nu: (be[i], 0, 0))
    bspec = pl.BlockSpec((1, 1, D_MODEL), lambda i, be, nu: (be[i], 0, 0))
    return pl.pallas_call(
        _expert_kernel,
        grid_spec=pltpu.PrefetchScalarGridSpec(
            num_scalar_prefetch=2,
            grid=(n_rows // bm,),
            in_specs=[pl.BlockSpec((bm, D_MODEL // 2), lambda i, be, nu: (i, 0)),
                      wspec, bspec, wspec, bspec, wspec, bspec],
            out_specs=pl.BlockSpec((bm, D_MODEL), lambda i, be, nu: (i, 0)),
            scratch_shapes=[pltpu.VMEM((D_MODEL, D_MODEL), BF16)] * 3,
        ),
        out_shape=jax.ShapeDtypeStruct((n_rows, D_MODEL), F32),
        compiler_params=_cparams("arbitrary"),
        name="moe_experts",
    )(block_e, n_used, x_rows, w_gate, b_gate, w_up, b_up, w_down, b_down)


def _combine_kernel(dcur_ref, dnext_ref, y_ref, p_ref, h_ref, g_ref, b_ref, o_ref, buf, sem):
    tc = h_ref.shape[0]
    i = pl.program_id(0)
    n = pl.num_programs(0)

    def issue(dref, slot):
        def body(t, _):
            for k in range(TOP_K):
                d = dref[t * TOP_K + k]
                pltpu.make_async_copy(y_ref.at[pl.ds(d, 1), :], buf.at[slot, k, pl.ds(t, 1), :], sem.at[slot]).start()
            return 0
        lax.fori_loop(0, tc, body, 0)

    slot = i % 2

    @pl.when(i == 0)
    def _():
        issue(dcur_ref, 0)

    @pl.when(i + 1 < n)
    def _():
        issue(dnext_ref, 1 - slot)

    def wait_body(t, _):
        for k in range(TOP_K):
            pltpu.make_async_copy(y_ref.at[pl.ds(0, 1), :], buf.at[slot, k, pl.ds(0, 1), :], sem.at[slot]).wait()
        return 0

    lax.fori_loop(0, tc, wait_body, 0)

    p = p_ref[...]
    y = p[:, 0:1] * buf[slot, 0]
    for k in range(1, TOP_K):
        y = y + p[:, k:k + 1] * buf[slot, k]
    o_ref[...] = _layer_norm(DN_ALPHA * h_ref[...] + y, g_ref[...], b_ref[...])


def _combine(dest_flat, y_rows, probs, h2, ln_g, ln_b, tc=128):
    T = h2.shape[0]
    nt = T // tc
    full = lambda shape: pl.BlockSpec(shape, lambda i: (0,) * len(shape))
    return pl.pallas_call(
        _combine_kernel,
        grid=(nt,),
        in_specs=[
            pl.BlockSpec((tc * TOP_K,), lambda i: (i,), memory_space=pltpu.SMEM),
            pl.BlockSpec((tc * TOP_K,), lambda i: (jnp.minimum(i + 1, nt - 1),), memory_space=pltpu.SMEM),
            pl.BlockSpec(memory_space=pl.ANY),
            pl.BlockSpec((tc, LANES), lambda i: (i, 0)),
            pl.BlockSpec((tc, D_MODEL), lambda i: (i, 0)),
            full((1, D_MODEL)), full((1, D_MODEL)),
        ],
        out_specs=pl.BlockSpec((tc, D_MODEL), lambda i: (i, 0)),
        out_shape=jax.ShapeDtypeStruct((T, D_MODEL), F32),
        scratch_shapes=[pltpu.VMEM((2, TOP_K, tc, D_MODEL), F32), pltpu.SemaphoreType.DMA((2,))],
        compiler_params=_cparams("arbitrary"),
        name="moe_combine",
    )(dest_flat, dest_flat, y_rows, probs, h2, ln_g, ln_b)


def _pad_lanes(v, fill=0.0):
    v = v.reshape(1, -1)
    return jnp.pad(v, ((0, 0), (0, LANES - v.shape[1])), constant_values=fill)


def _mixer_stage(x2, bsz, seq, ln_in_g, ln_in_b, w_in, b_branch_gate, conv_w, conv_b, dt_bias, a_log, d_skip,
                 ssd_norm_g, w_sb, w_ssd, w_mix_out, ln1_g, ln1_b):
    n_lin = 3 * D_MODEL + SSD_INNER + SSD_CONV_CH
    w_main = jnp.concatenate([w_in[:, :n_lin], w_in[:, n_lin + SSD_HEADS:]], axis=1).astype(BF16)
    w_dt = jnp.pad(w_in[:, n_lin:n_lin + SSD_HEADS], ((0, 0), (0, LANES - SSD_HEADS))).astype(BF16)
    h, qkv, z, xbc, gate_logits, dt_raw = _inproj(x2, ln_in_g.reshape(1, -1), ln_in_b.reshape(1, -1), w_main, w_dt)

    o_sb = _stick_breaking(qkv, bsz, seq)

    xbc_act = _conv_silu(xbc, conv_w, conv_b.reshape(1, -1), bsz, seq)
    a = -jnp.exp(a_log.astype(F32))
    o_ssd = _ssd(xbc_act, z, dt_raw, _pad_lanes(dt_bias), _pad_lanes(a),
                 jnp.repeat(d_skip, SSD_HEAD_DIM).reshape(1, -1), ssd_norm_g.reshape(1, -1), bsz, seq)

    return _merge(o_sb, o_ssd, gate_logits, h, b_branch_gate.reshape(1, -1), w_sb.astype(BF16), w_ssd.astype(BF16),
                  w_mix_out.astype(BF16), ln1_g.reshape(1, -1), ln1_b.reshape(1, -1))


def _xattn_stage(h1, mem2, bsz, seq, w_xq, w_xk, w_xv, w_xo, ln2_g, ln2_b):
    k, v = _kv_proj(mem2, w_xk.astype(BF16), w_xv.astype(BF16))
    return _xattn(h1, k, v, w_xq.astype(BF16), w_xo.astype(BF16), ln2_g.reshape(1, -1), ln2_b.reshape(1, -1),
                  bsz, seq)


def _moe_stage(h2, h2_packed, w_router, b_router, w_e_gate, b_e_gate, w_e_up, b_e_up, w_e_down, b_e_down, ln3_g,
               ln3_b):
    T = h2.shape[0]
    bm = EXPERT_ROWS
    w_r = jnp.pad(w_router, ((0, 0), (0, LANES - N_EXPERTS)))
    idx_p, probs, rank_p, counts_p = _router(h2, w_r, _pad_lanes(b_router))
    idx = idx_p[:, :TOP_K]
    rank = rank_p[:, :TOP_K]
    counts = counts_p[0, :N_EXPERTS].astype(jnp.int32)
    padded = (counts + bm - 1) // bm * bm
    end_padded = jnp.cumsum(padded)
    start_padded = end_padded - padded
    onehot = idx[:, :, None] == jnp.arange(N_EXPERTS, dtype=jnp.int32)[None, None, :]
    dest = jnp.sum(jnp.where(onehot, start_padded[None, None, :], 0), axis=-1) + rank
    dest_flat = dest.reshape(-1).astype(jnp.int32)
    n_blocks = -(-(T * TOP_K + N_EXPERTS * (bm - 1)) // bm)
    n_rows = n_blocks * bm
    blk_start = jnp.arange(n_blocks, dtype=jnp.int32) * bm
    block_e = jnp.minimum(jnp.sum(blk_start[:, None] >= end_padded[None, :], axis=-1), N_EXPERTS - 1).astype(jnp.int32)
    n_used = (end_padded[-1:] // bm).astype(jnp.int32)

    x_rows = _dispatch(dest_flat, h2_packed, n_rows)
    y_rows = _experts(block_e, n_used, x_rows, w_e_gate, b_e_gate[:, None, :], w_e_up, b_e_up[:, None, :],
                      w_e_down, b_e_down[:, None, :])
    return _combine(dest_flat, y_rows, probs, h2, ln3_g.reshape(1, -1), ln3_b.reshape(1, -1))


def kernel(x, mem, ln_in_g, ln_in_b, w_in, b_branch_gate, conv_w, conv_b, dt_bias, a_log, d_skip, ssd_norm_g, w_sb,
           w_ssd, w_mix_out, ln1_g, ln1_b, w_xq, w_xk, w_xv, w_xo, ln2_g, ln2_b, w_router, b_router, w_e_gate,
           b_e_gate, w_e_up, b_e_up, w_e_down, b_e_down, ln3_g, ln3_b):
    bsz, seq, _ = x.shape
    depth = w_in.shape[0]
    x2 = x.reshape(bsz * seq, D_MODEL)
    mem2 = mem.reshape(bsz * mem.shape[1], D_MODEL)
    assert depth == 1, "the entry LayerNorm is fused into the single layer's input projection"
    l = 0
    h1 = _mixer_stage(x2, bsz, seq, ln_in_g, ln_in_b, w_in[l], b_branch_gate[l], conv_w[l], conv_b[l], dt_bias[l],
                      a_log[l], d_skip[l], ssd_norm_g[l], w_sb[l], w_ssd[l], w_mix_out[l], ln1_g[l], ln1_b[l])
    h2, h2_packed = _xattn_stage(h1, mem2, bsz, seq, w_xq[l], w_xk[l], w_xv[l], w_xo[l], ln2_g[l], ln2_b[l])
    h = _moe_stage(h2, h2_packed, w_router[l], b_router[l], w_e_gate[l], b_e_gate[l], w_e_up[l], b_e_up[l], w_e_down[l],
                   b_e_down[l], ln3_g[l], ln3_b[l])
    return h.reshape(bsz, seq, D_MODEL)
```

```python
import functools

import jax
import jax.numpy as jnp
from jax import lax
from jax.experimental import pallas as pl
from jax.experimental.pallas import tpu as pltpu

F32 = jnp.float32
BF16 = jnp.bfloat16
HIGHEST = lax.Precision.HIGHEST

D_MODEL = 1024
LN_EPS = 1e-5
DN_ALPHA = 2.0 ** 0.25
LANES = 128

SB_HEADS = 16
SB_HEAD_DIM = 64
SB_TILE = 256
SB_LANES = 256
SB_GROUP = SB_LANES // SB_HEAD_DIM
LOG2E = 1.4426950408889634
SB_SKIP_BITS = 152.0

SSD_INNER = 2048
SSD_HEADS = 32
SSD_HEAD_DIM = 64
SSD_GROUPS = 4
SSD_GROUP_WIDTH = SSD_INNER // SSD_GROUPS
SSD_STATE = 128
SSD_CONV = 4
SSD_CONV_CH = SSD_INNER + 2 * SSD_GROUPS * SSD_STATE
SSD_CHUNK = 128

X_HEADS = 4
X_HEAD_DIM = 256
MEM_LEN = 256

N_EXPERTS = 32
TOP_K = 4
SWIGLU_LIMIT = 7.0
SWIGLU_ALPHA = 1.702
EXPERT_ROWS = 256

VMEM_LIMIT = 52 * 1024 * 1024


def _cparams(*sem):
    return pltpu.CompilerParams(dimension_semantics=sem, vmem_limit_bytes=VMEM_LIMIT)


def _layer_norm(x, g, b):
    mu = jnp.mean(x, axis=-1, keepdims=True)
    xc = x - mu
    var = jnp.mean(xc * xc, axis=-1, keepdims=True)
    return xc * lax.rsqrt(var + LN_EPS) * g + b


def _dot(a, b, precision=None):
    return jnp.dot(a, b, preferred_element_type=F32, precision=precision)


def _dot_nt(a, b, precision=None):
    return lax.dot_general(a, b, (((1,), (1,)), ((), ())), preferred_element_type=F32, precision=precision)


_IN_TN = 512
_N_QKV, _N_Z, _N_XBC, _N_GATE = (3 * D_MODEL // _IN_TN, SSD_INNER // _IN_TN, SSD_CONV_CH // _IN_TN,
                                 2 * D_MODEL // _IN_TN)
_QKV_SPLIT = D_MODEL // _IN_TN


def _inproj_kernel(x_ref, g_ref, b_ref, w_ref, wdt_ref, h_ref, qkv_ref, z_ref, xbc_ref, gate_ref, dt_ref, hb_ref):
    j = pl.program_id(1)

    @pl.when(j == 0)
    def _():
        h = _layer_norm(x_ref[...], g_ref[...], b_ref[...])
        h_ref[...] = h
        hb = h.astype(BF16)
        hb_ref[...] = hb
        dt_ref[...] = _dot(hb, wdt_ref[...])

    @pl.when(j < _N_QKV)
    def _():
        for c in range(_IN_TN // SB_LANES):
            sl = slice(c * SB_LANES, (c + 1) * SB_LANES)
            qkv_ref[0, c] = _dot(hb_ref[...], w_ref[:, sl]).astype(BF16)

    @pl.when((j >= _N_QKV) & (j < _N_QKV + _N_Z))
    def _():
        z_ref[...] = _dot(hb_ref[...], w_ref[...])

    @pl.when((j >= _N_QKV + _N_Z) & (j < _N_QKV + _N_Z + _N_XBC))
    def _():
        xbc_ref[...] = _dot(hb_ref[...], w_ref[...])

    @pl.when(j >= _N_QKV + _N_Z + _N_XBC)
    def _():
        gate_ref[...] = _dot(hb_ref[...], w_ref[...])


def _inproj(x2, ln_g, ln_b, w_main, w_dt, tm=1024):
    T = x2.shape[0]
    tm = min(tm, T)
    tn = _IN_TN
    nj = _N_QKV + _N_Z + _N_XBC + _N_GATE
    o_z, o_xbc, o_gate = _N_QKV, _N_QKV + _N_Z, _N_QKV + _N_Z + _N_XBC

    def qkv_map(i, j):
        jq = jnp.minimum(j, _N_QKV - 1)
        return (jq // _QKV_SPLIT, jq % _QKV_SPLIT, i, 0)

    return pl.pallas_call(
        _inproj_kernel,
        grid=(T // tm, nj),
        in_specs=[
            pl.BlockSpec((tm, D_MODEL), lambda i, j: (i, 0)),
            pl.BlockSpec((1, D_MODEL), lambda i, j: (0, 0)),
            pl.BlockSpec((1, D_MODEL), lambda i, j: (0, 0)),
            pl.BlockSpec((D_MODEL, tn), lambda i, j: (0, j)),
            pl.BlockSpec((D_MODEL, LANES), lambda i, j: (0, 0)),
        ],
        out_specs=[
            pl.BlockSpec((tm, D_MODEL), lambda i, j: (i, 0)),
            pl.BlockSpec((1, tn // SB_LANES, tm, SB_LANES), qkv_map),
            pl.BlockSpec((tm, tn), lambda i, j: (i, jnp.clip(j - o_z, 0, _N_Z - 1))),
            pl.BlockSpec((tm, tn), lambda i, j: (i, jnp.clip(j - o_xbc, 0, _N_XBC - 1))),
            pl.BlockSpec((tm, tn), lambda i, j: (i, jnp.clip(j - o_gate, 0, _N_GATE - 1))),
            pl.BlockSpec((tm, LANES), lambda i, j: (i, 0)),
        ],
        out_shape=[
            jax.ShapeDtypeStruct((T, D_MODEL), F32),
            jax.ShapeDtypeStruct((3, D_MODEL // SB_LANES, T, SB_LANES), BF16),
            jax.ShapeDtypeStruct((T, SSD_INNER), F32),
            jax.ShapeDtypeStruct((T, SSD_CONV_CH), F32),
            jax.ShapeDtypeStruct((T, 2 * D_MODEL), F32),
            jax.ShapeDtypeStruct((T, LANES), F32),
        ],
        scratch_shapes=[pltpu.VMEM((tm, D_MODEL), BF16)],
        compiler_params=_cparams("parallel", "arbitrary"),
        name="inproj",
    )(x2, ln_g, ln_b, w_main, w_dt)


def _sb_kernel(q_ref, k_ref, v_ref, o_ref):
    t = SB_TILE
    G = SB_GROUP
    i = pl.program_id(2)
    lane = lax.broadcasted_iota(jnp.int32, (t, SB_LANES), 1)
    head_lanes = [(lane >= h * SB_HEAD_DIM) & (lane < (h + 1) * SB_HEAD_DIM) for h in range(G)]
    qs = q_ref[0, 0] * BF16(SB_HEAD_DIM ** -0.5)
    zero = jnp.zeros_like(qs)
    q_stack = jnp.concatenate([jnp.where(m, qs, zero) for m in head_lanes], axis=0)
    row = lax.broadcasted_iota(jnp.int32, (G * t, t), 0) & (t - 1)
    col = lax.broadcasted_iota(jnp.int32, (G * t, t), 1)
    causal = col < row
    krow = lax.broadcasted_iota(jnp.int32, (2 * t, t), 0) & (t - 1)
    kcol = lax.broadcasted_iota(jnp.int32, (2 * t, t), 1)
    later2 = (krow > kcol).astype(BF16)

    def block(jb, c, acc, diag):
        k = k_ref[0, 0, pl.ds(pl.multiple_of(jb * t, t), t), :]
        v = v_ref[0, 0, pl.ds(pl.multiple_of(jb * t, t), t), :]
        vzero = jnp.zeros_like(v)
        v_stack = jnp.concatenate([jnp.where(m, v, vzero) for m in head_lanes], axis=0)
        z2 = _dot_nt(q_stack, k) * LOG2E
        neg_abs = lax.bitcast_convert_type(lax.bitcast_convert_type(z2, jnp.uint32) | jnp.uint32(0x80000000), F32)
        sp2 = jnp.maximum(z2, 0.0) + jnp.log2(1.0 + jnp.exp2(neg_abs))
        spm = jnp.where(causal, sp2, 0.0) if diag else sp2
        hi = spm.astype(BF16)
        lo = (spm - hi.astype(F32)).astype(BF16)
        after = _dot(jnp.concatenate([hi, lo], axis=1), later2)
        w = jnp.exp2(z2 - sp2 - after - c)
        if diag:
            w = jnp.where(causal, w, 0.0)
        wb = w.astype(BF16)
        w_cat = jnp.concatenate([wb[h * t:(h + 1) * t] for h in range(G)], axis=1)
        return c + jnp.sum(spm, axis=1, keepdims=True), acc + _dot(w_cat, v_stack)

    c, acc = block(i, jnp.zeros((G * t, 1), F32), jnp.zeros((t, SB_LANES), F32), True)

    def cond(carry):
        jb, cmin, _, _ = carry
        return (jb >= 0) & (cmin < SB_SKIP_BITS)

    def body(carry):
        jb, _, c, acc = carry
        c, acc = block(jb, c, acc, False)
        return jb - 1, jnp.min(c), c, acc

    _, _, _, acc = lax.while_loop(cond, body, (i - 1, jnp.min(c), c, acc))
    o_ref[...] = acc.astype(BF16)


def _stick_breaking(qkv, bsz, seq):
    t = SB_TILE
    nq = seq // t
    T = bsz * seq
    return pl.pallas_call(
        _sb_kernel,
        grid=(bsz, SB_HEADS // SB_GROUP, nq),
        in_specs=[
            pl.BlockSpec((1, 1, t, SB_LANES), lambda b, c, i: (0, c, b * nq + i, 0)),
            pl.BlockSpec((1, 1, seq, SB_LANES), lambda b, c, i: (1, c, b, 0)),
            pl.BlockSpec((1, 1, seq, SB_LANES), lambda b, c, i: (2, c, b, 0)),
        ],
        out_specs=pl.BlockSpec((t, SB_LANES), lambda b, c, i: (b * nq + i, c)),
        out_shape=jax.ShapeDtypeStruct((T, SB_HEADS * SB_HEAD_DIM), BF16),
        compiler_params=_cparams("parallel", "parallel", "arbitrary"),
        name="stick_breaking",
    )(qkv, qkv, qkv)


_CONV_HALO = 8


def _conv_kernel(u_ref, w_ref, b_ref, o_ref, ext_ref):
    ts = u_ref.shape[0]
    s = pl.program_id(1)

    @pl.when(s == 0)
    def _():
        ext_ref[0:_CONV_HALO, :] = jnp.zeros((_CONV_HALO, SSD_CONV_CH), F32)

    @pl.when(s > 0)
    def _():
        ext_ref[0:_CONV_HALO, :] = ext_ref[ts:ts + _CONV_HALO, :]

    ext_ref[_CONV_HALO:_CONV_HALO + ts, :] = u_ref[...]
    cw = 512
    for c0 in range(0, SSD_CONV_CH, cw):
        acc = jnp.broadcast_to(b_ref[:, c0:c0 + cw], (ts, cw))
        for k in range(SSD_CONV):
            off = _CONV_HALO - (SSD_CONV - 1) + k
            acc = acc + w_ref[k:k + 1, c0:c0 + cw] * ext_ref[off:off + ts, c0:c0 + cw]
        o_ref[:, c0:c0 + cw] = acc * jax.nn.sigmoid(acc)


def _conv_silu(xbc, conv_w, conv_b, bsz, seq, ts=256):
    T = bsz * seq
    ns = seq // ts
    return pl.pallas_call(
        _conv_kernel,
        grid=(bsz, ns),
        in_specs=[
            pl.BlockSpec((ts, SSD_CONV_CH), lambda b, s: (b * ns + s, 0)),
            pl.BlockSpec((SSD_CONV, SSD_CONV_CH), lambda b, s: (0, 0)),
            pl.BlockSpec((1, SSD_CONV_CH), lambda b, s: (0, 0)),
        ],
        out_specs=pl.BlockSpec((ts, SSD_CONV_CH), lambda b, s: (b * ns + s, 0)),
        out_shape=jax.ShapeDtypeStruct((T, SSD_CONV_CH), F32),
        scratch_shapes=[pltpu.VMEM((ts + _CONV_HALO, SSD_CONV_CH), F32)],
        compiler_params=_cparams("parallel", "arbitrary"),
        name="conv_silu",
    )(xbc, conv_w, conv_b)


def _ssd_kernel(xs_ref, bm_ref, cm_ref, z_ref, dtr_ref, dtb_ref, a128_ref, dexp_ref, ng_ref, e_ref,
                o_ref, state_ref, y_ref):
    L = SSD_CHUNK
    c = pl.program_id(1)

    @pl.when(c == 0)
    def _():
        state_ref[...] = jnp.zeros_like(state_ref)

    row = lax.broadcasted_iota(jnp.int32, (L, L), 0)
    col = lax.broadcasted_iota(jnp.int32, (L, L), 1)
    incl = (col <= row).astype(F32)
    eye = (col == row).astype(F32)
    lower = col <= row
    lane = lax.broadcasted_iota(jnp.int32, (L, LANES), 1)

    def expand(v):
        hi = v.astype(BF16)
        r1 = v - hi.astype(F32)
        mid = r1.astype(BF16)
        lo = (r1 - mid.astype(F32)).astype(BF16)
        return _dot(jnp.concatenate([hi, mid, lo], axis=1), e_ref[...])

    dt = jax.nn.softplus(dtr_ref[...] + dtb_ref[...])
    dt_exp = expand(dt)
    cs_col = _dot(incl, dt * a128_ref[...], HIGHEST)
    cs_exp = expand(cs_col)
    cs_row = _dot_nt(eye, cs_col, HIGHEST)

    xs = xs_ref[...]
    xdt = xs * dt_exp
    xdt_b = xdt.astype(BF16)
    cs_last = cs_exp[L - 1:L, :]
    xw_b = (xdt * jnp.exp(cs_last - cs_exp)).astype(BF16)
    grow = jnp.exp(cs_exp)
    grow_last = jnp.exp(cs_last)

    gw = SSD_GROUP_WIDTH
    heads_per_group = SSD_HEADS // SSD_GROUPS
    for g in range(SSD_GROUPS):
        bg = bm_ref[:, g * SSD_STATE:(g + 1) * SSD_STATE]
        cg = cm_ref[:, g * SSD_STATE:(g + 1) * SSD_STATE].astype(BF16)
        cb = _dot_nt(cg, bg.astype(BF16))
        for pair in range(heads_per_group // 2):
            c0 = g * gw + pair * LANES
            x_pair = xdt_b[:, c0:c0 + LANES]
            ys = []
            for sub in range(2):
                hd = g * heads_per_group + pair * 2 + sub
                diff = cs_col[:, hd:hd + 1] - cs_row[hd:hd + 1, :]
                seg = jnp.where(lower, jnp.exp(jnp.minimum(diff, 0.0)), 0.0)
                ys.append(_dot((cb * seg).astype(BF16), x_pair))
            y_ref[:, c0:c0 + LANES] = jnp.where(lane < SSD_HEAD_DIM, ys[0], ys[1])
        st = state_ref[g]
        y_ref[:, g * gw:(g + 1) * gw] += _dot(cg, st.astype(BF16)) * grow[:, g * gw:(g + 1) * gw]
        state_ref[g] = st * grow_last[:, g * gw:(g + 1) * gw] + _dot(bg.T.astype(BF16), xw_b[:, g * gw:(g + 1) * gw])

    z = z_ref[...]
    y = (y_ref[...] + dexp_ref[...] * xs) * (z * jax.nn.sigmoid(z))
    for g in range(SSD_GROUPS):
        yg = y[:, g * gw:(g + 1) * gw]
        ms = jnp.mean(yg * yg, axis=-1, keepdims=True)
        o_ref[:, g * gw:(g + 1) * gw] = (yg * lax.rsqrt(ms + LN_EPS) * ng_ref[:, g * gw:(g + 1) * gw]).astype(BF16)


def _ssd(xbc_act, z, dt_raw, dt_bias128, a128, d_exp, norm_g, bsz, seq):
    L = SSD_CHUNK
    T = bsz * seq
    nc = seq // L
    nb = SSD_INNER // SSD_GROUP_WIDTH
    head_of_row = jnp.arange(3 * LANES) % LANES
    expand = (head_of_row[:, None] == (jnp.arange(SSD_INNER)[None, :] // SSD_HEAD_DIM)).astype(BF16)
    full = lambda shape: pl.BlockSpec(shape, lambda b, c: (0,) * len(shape))
    return pl.pallas_call(
        _ssd_kernel,
        grid=(bsz, nc),
        in_specs=[
            pl.BlockSpec((L, SSD_INNER), lambda b, c: (b * nc + c, 0)),
            pl.BlockSpec((L, SSD_GROUPS * SSD_STATE), lambda b, c: (b * nc + c, nb)),
            pl.BlockSpec((L, SSD_GROUPS * SSD_STATE), lambda b, c: (b * nc + c, nb + 1)),
            pl.BlockSpec((L, SSD_INNER), lambda b, c: (b * nc + c, 0)),
            pl.BlockSpec((L, LANES), lambda b, c: (b * nc + c, 0)),
            full((1, LANES)), full((1, LANES)), full((1, SSD_INNER)), full((1, SSD_INNER)),
            full((3 * LANES, SSD_INNER)),
        ],
        out_specs=pl.BlockSpec((L, SSD_INNER), lambda b, c: (b * nc + c, 0)),
        out_shape=jax.ShapeDtypeStruct((T, SSD_INNER), BF16),
        scratch_shapes=[
            pltpu.VMEM((SSD_GROUPS, SSD_STATE, SSD_GROUP_WIDTH), F32),
            pltpu.VMEM((L, SSD_INNER), F32),
        ],
        compiler_params=_cparams("parallel", "arbitrary"),
        name="ssd_scan",
    )(xbc_act, xbc_act, xbc_act, z, dt_raw, dt_bias128, a128, d_exp, norm_g, expand)


def _merge_kernel(osb_ref, ossd_ref, gl_ref, h_ref, bg_ref, wsb_ref, wssd_ref, wmix_ref, g_ref, b_ref, o_ref):
    gates = jax.nn.sigmoid(gl_ref[...] + bg_ref[...])
    merged = (gates[:, :D_MODEL] * _dot(osb_ref[...], wsb_ref[...])
              + gates[:, D_MODEL:] * _dot(ossd_ref[...], wssd_ref[...]))
    mix = _dot(merged.astype(BF16), wmix_ref[...])
    o_ref[...] = _layer_norm(DN_ALPHA * h_ref[...] + mix, g_ref[...], b_ref[...])


def _merge(o_sb, o_ssd, gate_logits, h, b_gate, w_sb, w_ssd, w_mix, ln_g, ln_b, tm=256):
    T = h.shape[0]
    full = lambda shape: pl.BlockSpec(shape, lambda i: (0,) * len(shape))
    return pl.pallas_call(
        _merge_kernel,
        grid=(T // tm,),
        in_specs=[
            pl.BlockSpec((tm, D_MODEL), lambda i: (i, 0)),
            pl.BlockSpec((tm, SSD_INNER), lambda i: (i, 0)),
            pl.BlockSpec((tm, 2 * D_MODEL), lambda i: (i, 0)),
            pl.BlockSpec((tm, D_MODEL), lambda i: (i, 0)),
            full((1, 2 * D_MODEL)), full((D_MODEL, D_MODEL)), full((SSD_INNER, D_MODEL)), full((D_MODEL, D_MODEL)),
            full((1, D_MODEL)), full((1, D_MODEL)),
        ],
        out_specs=pl.BlockSpec((tm, D_MODEL), lambda i: (i, 0)),
        out_shape=jax.ShapeDtypeStruct((T, D_MODEL), F32),
        compiler_params=_cparams("parallel"),
        name="merge",
    )(o_sb, o_ssd, gate_logits, h, b_gate, w_sb, w_ssd, w_mix, ln_g, ln_b)


def _kv_kernel(m_ref, wk_ref, wv_ref, k_ref, v_ref):
    mb = m_ref[...].astype(BF16)
    k_ref[...] = _dot(mb, wk_ref[...]).astype(BF16)
    v_ref[...] = _dot(mb, wv_ref[...]).astype(BF16)


def _kv_proj(mem2, w_xk, w_xv, tm=512):
    M = mem2.shape[0]
    full = lambda shape: pl.BlockSpec(shape, lambda i: (0,) * len(shape))
    return pl.pallas_call(
        _kv_kernel,
        grid=(M // tm,),
        in_specs=[pl.BlockSpec((tm, D_MODEL), lambda i: (i, 0)), full((D_MODEL, D_MODEL)), full((D_MODEL, D_MODEL))],
        out_specs=[pl.BlockSpec((tm, D_MODEL), lambda i: (i, 0))] * 2,
        out_shape=[jax.ShapeDtypeStruct((M, D_MODEL), BF16)] * 2,
        compiler_params=_cparams("parallel"),
        name="kv_proj",
    )(mem2, w_xk, w_xv)


def _pack_bf16_pairs(x):
    bits = lax.bitcast_convert_type(x.astype(BF16).astype(F32), jnp.uint32)
    half = x.shape[1] // 2
    return (bits[:, :half] >> 16) | (bits[:, half:] & jnp.uint32(0xFFFF0000))


def _unpack_bf16_pairs(u):
    lo = lax.bitcast_convert_type(u << 16, F32)
    hi = lax.bitcast_convert_type(u & jnp.uint32(0xFFFF0000), F32)
    return jnp.concatenate([lo, hi], axis=1).astype(BF16)


def _xattn_kernel(h_ref, k_ref, v_ref, wq_ref, wo_ref, g_ref, b_ref, o_ref, op_ref):
    h = h_ref[...]
    q = (_dot(h.astype(BF16), wq_ref[...]) * (X_HEAD_DIM ** -0.5)).astype(BF16)
    outs = []
    for hd in range(X_HEADS):
        sl = slice(hd * X_HEAD_DIM, (hd + 1) * X_HEAD_DIM)
        s = _dot_nt(q[:, sl], k_ref[:, sl])
        e = jnp.exp(s - jnp.max(s, axis=-1, keepdims=True))
        p = e / jnp.sum(e, axis=-1, keepdims=True)
        outs.append(_dot(p.astype(BF16), v_ref[:, sl]))
    o = jnp.concatenate(outs, axis=-1).astype(BF16)
    h2 = _layer_norm(DN_ALPHA * h + _dot(o, wo_ref[...]), g_ref[...], b_ref[...])
    o_ref[...] = h2
    op_ref[...] = _pack_bf16_pairs(h2)


def _xattn(h1, k, v, w_xq, w_xo, ln_g, ln_b, bsz, seq, tm=256):
    T = bsz * seq
    nt = seq // tm
    full = lambda shape: pl.BlockSpec(shape, lambda b, i: (0,) * len(shape))
    return pl.pallas_call(
        _xattn_kernel,
        grid=(bsz, nt),
        in_specs=[
            pl.BlockSpec((tm, D_MODEL), lambda b, i: (b * nt + i, 0)),
            pl.BlockSpec((MEM_LEN, D_MODEL), lambda b, i: (b, 0)),
            pl.BlockSpec((MEM_LEN, D_MODEL), lambda b, i: (b, 0)),
            full((D_MODEL, D_MODEL)), full((D_MODEL, D_MODEL)), full((1, D_MODEL)), full((1, D_MODEL)),
        ],
        out_specs=[pl.BlockSpec((tm, D_MODEL), lambda b, i: (b * nt + i, 0)),
                   pl.BlockSpec((tm, D_MODEL // 2), lambda b, i: (b * nt + i, 0))],
        out_shape=[jax.ShapeDtypeStruct((T, D_MODEL), F32), jax.ShapeDtypeStruct((T, D_MODEL // 2), jnp.uint32)],
        compiler_params=_cparams("parallel", "parallel"),
        name="xattn",
    )(h1, k, v, w_xq, w_xo, ln_g, ln_b)


def _router_kernel(h_ref, w_ref, b_ref, idx_ref, p_ref, rank_ref, cnt_ref, run_ref):
    tm = h_ref.shape[0]
    i = pl.program_id(0)

    @pl.when(i == 0)
    def _():
        run_ref[...] = jnp.zeros_like(run_ref)

    lane = lax.broadcasted_iota(jnp.int32, (tm, LANES), 1)
    lane_f = lane.astype(F32)
    neg = jnp.float32(-jnp.inf)
    logits = _dot(h_ref[...], w_ref[...], HIGHEST) + b_ref[...]
    work = jnp.where(lane < N_EXPERTS, logits, neg)
    vals, hots = [], []
    idx_out = jnp.zeros((tm, LANES), F32)
    for k in range(TOP_K):
        m = jnp.max(work, axis=-1, keepdims=True)
        first = jnp.min(jnp.where(work == m, lane_f, float(LANES)), axis=-1, keepdims=True)
        hot = lane_f == first
        vals.append(m)
        hots.append(hot)
        idx_out = jnp.where(lane == k, first, idx_out)
        work = jnp.where(hot, neg, work)
    exps = [jnp.exp(v - vals[0]) for v in vals]
    denom = exps[0] + exps[1] + exps[2] + exps[3]
    member = jnp.zeros((tm, LANES), F32)
    p_out = jnp.zeros((tm, LANES), F32)
    for k in range(TOP_K):
        member = jnp.where(hots[k], 1.0, member)
        p_out = jnp.where(lane == k, exps[k] / denom, p_out)
    row = lax.broadcasted_iota(jnp.int32, (tm, tm), 0)
    col = lax.broadcasted_iota(jnp.int32, (tm, tm), 1)
    before = (col < row).astype(BF16)
    earlier = _dot(before, member.astype(BF16)) + run_ref[...]
    rank_out = jnp.zeros((tm, LANES), F32)
    for k in range(TOP_K):
        r = jnp.sum(jnp.where(hots[k], earlier, 0.0), axis=-1, keepdims=True)
        rank_out = jnp.where(lane == k, r, rank_out)
    run = run_ref[...] + jnp.sum(member, axis=0, keepdims=True)
    run_ref[...] = run
    cnt_ref[...] = run
    idx_ref[...] = idx_out.astype(jnp.int32)
    p_ref[...] = p_out
    rank_ref[...] = rank_out.astype(jnp.int32)


def _router(h2, w_r, b_r, tm=512):
    T = h2.shape[0]
    full = lambda shape: pl.BlockSpec(shape, lambda i: (0,) * len(shape))
    tok = pl.BlockSpec((tm, LANES), lambda i: (i, 0))
    return pl.pallas_call(
        _router_kernel,
        grid=(T // tm,),
        in_specs=[pl.BlockSpec((tm, D_MODEL), lambda i: (i, 0)), full((D_MODEL, LANES)), full((1, LANES))],
        out_specs=[tok, tok, tok, full((1, LANES))],
        out_shape=[
            jax.ShapeDtypeStruct((T, LANES), jnp.int32),
            jax.ShapeDtypeStruct((T, LANES), F32),
            jax.ShapeDtypeStruct((T, LANES), jnp.int32),
            jax.ShapeDtypeStruct((1, LANES), F32),
        ],
        scratch_shapes=[pltpu.VMEM((1, LANES), F32)],
        compiler_params=_cparams("arbitrary"),
        name="router",
    )(h2, w_r, b_r)


def _dispatch_kernel(dest_ref, h_ref, xin_ref, x_ref, sem):
    del xin_ref
    td = h_ref.shape[0]

    def issue(t, _):
        for k in range(TOP_K):
            d = dest_ref[t * TOP_K + k]
            pltpu.make_async_copy(h_ref.at[pl.ds(t, 1), :], x_ref.at[pl.ds(d, 1), :], sem).start(priority=k % 2)
        return 0

    lax.fori_loop(0, td, issue, 0)

    def wait(t, _):
        for k in range(TOP_K):
            pltpu.make_async_copy(h_ref.at[pl.ds(0, 1), :], x_ref.at[pl.ds(0, 1), :], sem).wait()
        return 0

    lax.fori_loop(0, td, wait, 0, unroll=8)


def _dispatch(dest_flat, h2_packed, n_rows, td=256):
    T, width = h2_packed.shape
    x0 = jnp.zeros((n_rows, width), h2_packed.dtype)
    return pl.pallas_call(
        _dispatch_kernel,
        grid=(T // td,),
        in_specs=[
            pl.BlockSpec((td * TOP_K,), lambda i: (i,), memory_space=pltpu.SMEM),
            pl.BlockSpec((td, width), lambda i: (i, 0)),
            pl.BlockSpec(memory_space=pl.ANY),
        ],
        out_specs=pl.BlockSpec(memory_space=pl.ANY),
        out_shape=jax.ShapeDtypeStruct((n_rows, width), h2_packed.dtype),
        scratch_shapes=[pltpu.SemaphoreType.DMA(())],
        input_output_aliases={2: 0},
        compiler_params=_cparams("arbitrary"),
        name="moe_dispatch",
    )(dest_flat, h2_packed, x0)


def _expert_kernel(be_ref, nused_ref, x_ref, wg_ref, bg_ref, wu_ref, bu_ref, wd_ref, bd_ref, y_ref,
                   wg_bf, wu_bf, wd_bf):
    blk = pl.program_id(0)
    new_expert = (blk == 0) | (be_ref[blk] != be_ref[jnp.maximum(blk - 1, 0)])

    @pl.when(new_expert & (blk < nused_ref[0]))
    def _():
        wg_bf[...] = wg_ref[0].astype(BF16)
        wu_bf[...] = wu_ref[0].astype(BF16)
        wd_bf[...] = wd_ref[0].astype(BF16)

    @pl.when(blk < nused_ref[0])
    def _():
        xb = _unpack_bf16_pairs(x_ref[...])
        g = jnp.minimum(_dot(xb, wg_bf[...]) + bg_ref[0], SWIGLU_LIMIT)
        u = jnp.clip(_dot(xb, wu_bf[...]) + bu_ref[0], -SWIGLU_LIMIT, SWIGLU_LIMIT)
        act = (u + 1.0) * g * jax.nn.sigmoid(SWIGLU_ALPHA * g)
        y_ref[...] = _dot(act.astype(BF16), wd_bf[...]) + bd_ref[0]

    @pl.when(blk >= nused_ref[0])
    def _():
        y_ref[...] = jnp.zeros_like(y_ref)


def _experts(block_e, n_used, x_rows, w_gate, b_gate, w_up, b_up, w_down, b_down):
    n_rows = x_rows.shape[0]
    bm = EXPERT_ROWS
    wspec = pl.BlockSpec((1, D_MODEL, D_MODEL), lambda i, be, nu: (be[i], 0, 0))
    bspec = pl.BlockSpec((1, 1, D_MODEL), lambda i, be, nu: (be[i], 0, 0))
    return pl.pallas_call(
        _expert_kernel,
        grid_spec=pltpu.PrefetchScalarGridSpec(
            num_scalar_prefetch=2,
            grid=(n_rows // bm,),
            in_specs=[pl.BlockSpec((bm, D_MODEL // 2), lambda i, be, nu: (i, 0)),
                      wspec, bspec, wspec, bspec, wspec, bspec],
            out_specs=pl.BlockSpec((bm, D_MODEL), lambda i, be, nu: (i, 0)),
            scratch_shapes=[pltpu.VMEM((D_MODEL, D_MODEL), BF16)] * 3,
        ),
        out_shape=jax.ShapeDtypeStruct((n_rows, D_MODEL), F32),
        compiler_params=_cparams("arbitrary"),
        name="moe_experts",
    )(block_e, n_used, x_rows, w_gate, b_gate, w_up, b_up, w_down, b_down)


def _combine_kernel(dcur_ref, dnext_ref, y_ref, p_ref, h_ref, g_ref, b_ref, o_ref, buf, sem):
    tc = h_ref.shape[0]
    i = pl.program_id(0)
    n = pl.num_programs(0)

    def issue(dref, slot):
        def body(t, _):
            for k in range(TOP_K):
                d = dref[t * TOP_K + k]
                pltpu.make_async_copy(y_ref.at[pl.ds(d, 1), :], buf.at[slot, k, pl.ds(t, 1), :],
                                      sem.at[slot]).start(priority=k % 2)
            return 0
        lax.fori_loop(0, tc, body, 0, unroll=4)

    slot = i % 2

    @pl.when(i == 0)
    def _():
        issue(dcur_ref, 0)

    @pl.when(i + 1 < n)
    def _():
        issue(dnext_ref, 1 - slot)

    def wait_body(t, _):
        for k in range(TOP_K):
            pltpu.make_async_copy(y_ref.at[pl.ds(0, 1), :], buf.at[slot, k, pl.ds(0, 1), :], sem.at[slot]).wait()
        return 0

    lax.fori_loop(0, tc, wait_body, 0, unroll=8)

    p = p_ref[...]
    y = p[:, 0:1] * buf[slot, 0]
    for k in range(1, TOP_K):
        y = y + p[:, k:k + 1] * buf[slot, k]
    o_ref[...] = _layer_norm(DN_ALPHA * h_ref[...] + y, g_ref[...], b_ref[...])


def _combine(dest_flat, y_rows, probs, h2, ln_g, ln_b, tc=128):
    T = h2.shape[0]
    nt = T // tc
    full = lambda shape: pl.BlockSpec(shape, lambda i: (0,) * len(shape))
    return pl.pallas_call(
        _combine_kernel,
        grid=(nt,),
        in_specs=[
            pl.BlockSpec((tc * TOP_K,), lambda i: (i,), memory_space=pltpu.SMEM),
            pl.BlockSpec((tc * TOP_K,), lambda i: (jnp.minimum(i + 1, nt - 1),), memory_space=pltpu.SMEM),
            pl.BlockSpec(memory_space=pl.ANY),
            pl.BlockSpec((tc, LANES), lambda i: (i, 0)),
            pl.BlockSpec((tc, D_MODEL), lambda i: (i, 0)),
            full((1, D_MODEL)), full((1, D_MODEL)),
        ],
        out_specs=pl.BlockSpec((tc, D_MODEL), lambda i: (i, 0)),
        out_shape=jax.ShapeDtypeStruct((T, D_MODEL), F32),
        scratch_shapes=[pltpu.VMEM((2, TOP_K, tc, D_MODEL), F32), pltpu.SemaphoreType.DMA((2,))],
        compiler_params=_cparams("arbitrary"),
        name="moe_combine",
    )(dest_flat, dest_flat, y_rows, probs, h2, ln_g, ln_b)


def _pad_lanes(v, fill=0.0):
    v = v.reshape(1, -1)
    return jnp.pad(v, ((0, 0), (0, LANES - v.shape[1])), constant_values=fill)


def _mixer_stage(x2, bsz, seq, ln_in_g, ln_in_b, w_in, b_branch_gate, conv_w, conv_b, dt_bias, a_log, d_skip,
                 ssd_norm_g, w_sb, w_ssd, w_mix_out, ln1_g, ln1_b):
    n_lin = 3 * D_MODEL + SSD_INNER + SSD_CONV_CH
    w_main = jnp.concatenate([w_in[:, :n_lin], w_in[:, n_lin + SSD_HEADS:]], axis=1).astype(BF16)
    w_dt = jnp.pad(w_in[:, n_lin:n_lin + SSD_HEADS], ((0, 0), (0, LANES - SSD_HEADS))).astype(BF16)
    h, qkv, z, xbc, gate_logits, dt_raw = _inproj(x2, ln_in_g.reshape(1, -1), ln_in_b.reshape(1, -1), w_main, w_dt)

    o_sb = _stick_breaking(qkv, bsz, seq)

    xbc_act = _conv_silu(xbc, conv_w, conv_b.reshape(1, -1), bsz, seq)
    a = -jnp.exp(a_log.astype(F32))
    o_ssd = _ssd(xbc_act, z, dt_raw, _pad_lanes(dt_bias), _pad_lanes(a),
                 jnp.repeat(d_skip, SSD_HEAD_DIM).reshape(1, -1), ssd_norm_g.reshape(1, -1), bsz, seq)

    return _merge(o_sb, o_ssd, gate_logits, h, b_branch_gate.reshape(1, -1), w_sb.astype(BF16), w_ssd.astype(BF16),
                  w_mix_out.astype(BF16), ln1_g.reshape(1, -1), ln1_b.reshape(1, -1))


def _xattn_stage(h1, mem2, bsz, seq, w_xq, w_xk, w_xv, w_xo, ln2_g, ln2_b):
    k, v = _kv_proj(mem2, w_xk.astype(BF16), w_xv.astype(BF16))
    return _xattn(h1, k, v, w_xq.astype(BF16), w_xo.astype(BF16), ln2_g.reshape(1, -1), ln2_b.reshape(1, -1),
                  bsz, seq)


def _moe_stage(h2, h2_packed, w_router, b_router, w_e_gate, b_e_gate, w_e_up, b_e_up, w_e_down, b_e_down, ln3_g,
               ln3_b):
    T = h2.shape[0]
    bm = EXPERT_ROWS
    w_r = jnp.pad(w_router, ((0, 0), (0, LANES - N_EXPERTS)))
    idx_p, probs, rank_p, counts_p = _router(h2, w_r, _pad_lanes(b_router))
    idx = idx_p[:, :TOP_K]
    rank = rank_p[:, :TOP_K]
    counts = counts_p[0, :N_EXPERTS].astype(jnp.int32)
    padded = (counts + bm - 1) // bm * bm
    end_padded = jnp.cumsum(padded)
    start_padded = end_padded - padded
    onehot = idx[:, :, None] == jnp.arange(N_EXPERTS, dtype=jnp.int32)[None, None, :]
    dest = jnp.sum(jnp.where(onehot, start_padded[None, None, :], 0), axis=-1) + rank
    dest_flat = dest.reshape(-1).astype(jnp.int32)
    n_blocks = -(-(T * TOP_K + N_EXPERTS * (bm - 1)) // bm)
    n_rows = n_blocks * bm
    blk_start = jnp.arange(n_blocks, dtype=jnp.int32) * bm
    block_e = jnp.minimum(jnp.sum(blk_start[:, None] >= end_padded[None, :], axis=-1), N_EXPERTS - 1).astype(jnp.int32)
    n_used = (end_padded[-1:] // bm).astype(jnp.int32)

    x_rows = _dispatch(dest_flat, h2_packed, n_rows)
    y_rows = _experts(block_e, n_used, x_rows, w_e_gate, b_e_gate[:, None, :], w_e_up, b_e_up[:, None, :],
                      w_e_down, b_e_down[:, None, :])
    return _combine(dest_flat, y_rows, probs, h2, ln3_g.reshape(1, -1), ln3_b.reshape(1, -1))


def kernel(x, mem, ln_in_g, ln_in_b, w_in, b_branch_gate, conv_w, conv_b, dt_bias, a_log, d_skip, ssd_norm_g, w_sb,
           w_ssd, w_mix_out, ln1_g, ln1_b, w_xq, w_xk, w_xv, w_xo, ln2_g, ln2_b, w_router, b_router, w_e_gate,
           b_e_gate, w_e_up, b_e_up, w_e_down, b_e_down, ln3_g, ln3_b):
    bsz, seq, _ = x.shape
    depth = w_in.shape[0]
    x2 = x.reshape(bsz * seq, D_MODEL)
    mem2 = mem.reshape(bsz * mem.shape[1], D_MODEL)
    assert depth == 1, "the entry LayerNorm is fused into the single layer's input projection"
    l = 0
    h1 = _mixer_stage(x2, bsz, seq, ln_in_g, ln_in_b, w_in[l], b_branch_gate[l], conv_w[l], conv_b[l], dt_bias[l],
                      a_log[l], d_skip[l], ssd_norm_g[l], w_sb[l], w_ssd[l], w_mix_out[l], ln1_g[l], ln1_b[l])
    h2, h2_packed = _xattn_stage(h1, mem2, bsz, seq, w_xq[l], w_xk[l], w_xv[l], w_xo[l], ln2_g[l], ln2_b[l])
    h = _moe_stage(h2, h2_packed, w_router[l], b_router[l], w_e_gate[l], b_e_gate[l], w_e_up[l], b_e_up[l], w_e_down[l],
                   b_e_down[l], ln3_g[l], ln3_b[l])
    return h.reshape(bsz, seq, D_MODEL)
```

```python
import functools

import jax
import jax.numpy as jnp
from jax import lax
from jax.experimental import pallas as pl
from jax.experimental.pallas import tpu as pltpu

F32 = jnp.float32
BF16 = jnp.bfloat16
HIGHEST = lax.Precision.HIGHEST

D_MODEL = 1024
LN_EPS = 1e-5
DN_ALPHA = 2.0 ** 0.25
LANES = 128

SB_HEADS = 16
SB_HEAD_DIM = 64
SB_TILE = 256
SB_LANES = 256
SB_GROUP = SB_LANES // SB_HEAD_DIM
LOG2E = 1.4426950408889634
SB_SKIP_BITS = 152.0

SSD_INNER = 2048
SSD_HEADS = 32
SSD_HEAD_DIM = 64
SSD_GROUPS = 4
SSD_GROUP_WIDTH = SSD_INNER // SSD_GROUPS
SSD_STATE = 128
SSD_CONV = 4
SSD_CONV_CH = SSD_INNER + 2 * SSD_GROUPS * SSD_STATE
SSD_CHUNK = 128

X_HEADS = 4
X_HEAD_DIM = 256
MEM_LEN = 256

N_EXPERTS = 32
TOP_K = 4
SWIGLU_LIMIT = 7.0
SWIGLU_ALPHA = 1.702
EXPERT_ROWS = 512

VMEM_LIMIT = 52 * 1024 * 1024


def _cparams(*sem):
    return pltpu.CompilerParams(dimension_semantics=sem, vmem_limit_bytes=VMEM_LIMIT)


def _layer_norm(x, g, b):
    mu = jnp.mean(x, axis=-1, keepdims=True)
    xc = x - mu
    var = jnp.mean(xc * xc, axis=-1, keepdims=True)
    return xc * lax.rsqrt(var + LN_EPS) * g + b


def _dot(a, b, precision=None):
    return jnp.dot(a, b, preferred_element_type=F32, precision=precision)


def _dot_nt(a, b, precision=None):
    return lax.dot_general(a, b, (((1,), (1,)), ((), ())), preferred_element_type=F32, precision=precision)


_IN_TN = 1024
_N_QKV, _N_Z, _N_XBC, _N_GATE = (3 * D_MODEL // _IN_TN, SSD_INNER // _IN_TN, SSD_CONV_CH // _IN_TN,
                                 2 * D_MODEL // _IN_TN)
_QKV_SPLIT = D_MODEL // _IN_TN


def _inproj_kernel(x_ref, g_ref, b_ref, w_ref, wdt_ref, h_ref, qkv_ref, z_ref, xbc_ref, gate_ref, dt_ref, hb_ref):
    j = pl.program_id(1)

    @pl.when(j == 0)
    def _():
        h = _layer_norm(x_ref[...], g_ref[...], b_ref[...])
        h_ref[...] = h
        hb = h.astype(BF16)
        hb_ref[...] = hb
        dt_ref[...] = _dot(hb, wdt_ref[...])

    @pl.when(j < _N_QKV)
    def _():
        for c in range(_IN_TN // SB_LANES):
            sl = slice(c * SB_LANES, (c + 1) * SB_LANES)
            qkv_ref[0, c] = _dot(hb_ref[...], w_ref[:, sl]).astype(BF16)

    @pl.when((j >= _N_QKV) & (j < _N_QKV + _N_Z))
    def _():
        z_ref[...] = _dot(hb_ref[...], w_ref[...]).astype(BF16)

    @pl.when((j >= _N_QKV + _N_Z) & (j < _N_QKV + _N_Z + _N_XBC))
    def _():
        xbc_ref[...] = _dot(hb_ref[...], w_ref[...]).astype(BF16)

    @pl.when(j >= _N_QKV + _N_Z + _N_XBC)
    def _():
        gate_ref[...] = _dot(hb_ref[...], w_ref[...]).astype(BF16)


def _inproj(x2, ln_g, ln_b, w_main, w_dt, tm=1024):
    T = x2.shape[0]
    tm = min(tm, T)
    tn = _IN_TN
    nj = _N_QKV + _N_Z + _N_XBC + _N_GATE
    o_z, o_xbc, o_gate = _N_QKV, _N_QKV + _N_Z, _N_QKV + _N_Z + _N_XBC

    def qkv_map(i, j):
        jq = jnp.minimum(j, _N_QKV - 1)
        return (jq // _QKV_SPLIT, jq % _QKV_SPLIT, i, 0)

    return pl.pallas_call(
        _inproj_kernel,
        grid=(T // tm, nj),
        in_specs=[
            pl.BlockSpec((tm, D_MODEL), lambda i, j: (i, 0)),
            pl.BlockSpec((1, D_MODEL), lambda i, j: (0, 0)),
            pl.BlockSpec((1, D_MODEL), lambda i, j: (0, 0)),
            pl.BlockSpec((D_MODEL, tn), lambda i, j: (0, j)),
            pl.BlockSpec((D_MODEL, LANES), lambda i, j: (0, 0)),
        ],
        out_specs=[
            pl.BlockSpec((tm, D_MODEL), lambda i, j: (i, 0)),
            pl.BlockSpec((1, tn // SB_LANES, tm, SB_LANES), qkv_map),
            pl.BlockSpec((tm, tn), lambda i, j: (i, jnp.clip(j - o_z, 0, _N_Z - 1))),
            pl.BlockSpec((tm, tn), lambda i, j: (i, jnp.clip(j - o_xbc, 0, _N_XBC - 1))),
            pl.BlockSpec((tm, tn), lambda i, j: (i, jnp.clip(j - o_gate, 0, _N_GATE - 1))),
            pl.BlockSpec((tm, LANES), lambda i, j: (i, 0)),
        ],
        out_shape=[
            jax.ShapeDtypeStruct((T, D_MODEL), F32),
            jax.ShapeDtypeStruct((3, D_MODEL // SB_LANES, T, SB_LANES), BF16),
            jax.ShapeDtypeStruct((T, SSD_INNER), BF16),
            jax.ShapeDtypeStruct((T, SSD_CONV_CH), BF16),
            jax.ShapeDtypeStruct((T, 2 * D_MODEL), BF16),
            jax.ShapeDtypeStruct((T, LANES), F32),
        ],
        scratch_shapes=[pltpu.VMEM((tm, D_MODEL), BF16)],
        compiler_params=_cparams("parallel", "arbitrary"),
        name="inproj",
    )(x2, ln_g, ln_b, w_main, w_dt)


def _sb_kernel(q_ref, k_ref, v_ref, o_ref):
    t = SB_TILE
    G = SB_GROUP
    i = pl.program_id(2)
    lane = lax.broadcasted_iota(jnp.int32, (t, SB_LANES), 1)
    head_lanes = [(lane >= h * SB_HEAD_DIM) & (lane < (h + 1) * SB_HEAD_DIM) for h in range(G)]
    qs = q_ref[0, 0] * BF16(SB_HEAD_DIM ** -0.5)
    zero = jnp.zeros_like(qs)
    q_stack = jnp.concatenate([jnp.where(m, qs, zero) for m in head_lanes], axis=0)
    row = lax.broadcasted_iota(jnp.int32, (G * t, t), 0) & (t - 1)
    col = lax.broadcasted_iota(jnp.int32, (G * t, t), 1)
    causal = col < row
    krow = lax.broadcasted_iota(jnp.int32, (2 * t, t), 0) & (t - 1)
    kcol = lax.broadcasted_iota(jnp.int32, (2 * t, t), 1)
    later2 = (krow > kcol).astype(BF16)

    def block(jb, c, acc, diag):
        k = k_ref[0, 0, pl.ds(pl.multiple_of(jb * t, t), t), :]
        v = v_ref[0, 0, pl.ds(pl.multiple_of(jb * t, t), t), :]
        vzero = jnp.zeros_like(v)
        v_stack = jnp.concatenate([jnp.where(m, v, vzero) for m in head_lanes], axis=0)
        z2 = _dot_nt(q_stack, k) * LOG2E
        neg_abs = lax.bitcast_convert_type(lax.bitcast_convert_type(z2, jnp.uint32) | jnp.uint32(0x80000000), F32)
        sp2 = jnp.maximum(z2, 0.0) + jnp.log2(1.0 + jnp.exp2(neg_abs))
        spm = jnp.where(causal, sp2, 0.0) if diag else sp2
        hi = spm.astype(BF16)
        lo = (spm - hi.astype(F32)).astype(BF16)
        after = _dot(jnp.concatenate([hi, lo], axis=1), later2)
        w = jnp.exp2(z2 - sp2 - after - c)
        if diag:
            w = jnp.where(causal, w, 0.0)
        wb = w.astype(BF16)
        w_cat = jnp.concatenate([wb[h * t:(h + 1) * t] for h in range(G)], axis=1)
        return c + jnp.sum(spm, axis=1, keepdims=True), acc + _dot(w_cat, v_stack)

    c, acc = block(i, jnp.zeros((G * t, 1), F32), jnp.zeros((t, SB_LANES), F32), True)

    def cond(carry):
        jb, cmin, _, _ = carry
        return (jb >= 0) & (cmin < SB_SKIP_BITS)

    def body(carry):
        jb, _, c, acc = carry
        c, acc = block(jb, c, acc, False)
        return jb - 1, jnp.min(c), c, acc

    _, _, _, acc = lax.while_loop(cond, body, (i - 1, jnp.min(c), c, acc))
    o_ref[...] = acc.astype(BF16)


def _stick_breaking(qkv, bsz, seq):
    t = SB_TILE
    nq = seq // t
    T = bsz * seq
    return pl.pallas_call(
        _sb_kernel,
        grid=(bsz, SB_HEADS // SB_GROUP, nq),
        in_specs=[
            pl.BlockSpec((1, 1, t, SB_LANES), lambda b, c, i: (0, c, b * nq + i, 0)),
            pl.BlockSpec((1, 1, seq, SB_LANES), lambda b, c, i: (1, c, b, 0)),
            pl.BlockSpec((1, 1, seq, SB_LANES), lambda b, c, i: (2, c, b, 0)),
        ],
        out_specs=pl.BlockSpec((t, SB_LANES), lambda b, c, i: (b * nq + i, c)),
        out_shape=jax.ShapeDtypeStruct((T, SB_HEADS * SB_HEAD_DIM), BF16),
        compiler_params=_cparams("parallel", "parallel", "arbitrary"),
        name="stick_breaking",
    )(qkv, qkv, qkv)


_CONV_HALO = 8


def _conv_kernel(u_ref, w_ref, b_ref, o_ref, ext_ref):
    ts = u_ref.shape[0]
    s = pl.program_id(1)

    @pl.when(s == 0)
    def _():
        ext_ref[0:_CONV_HALO, :] = jnp.zeros((_CONV_HALO, SSD_CONV_CH), F32)

    @pl.when(s > 0)
    def _():
        ext_ref[0:_CONV_HALO, :] = ext_ref[ts:ts + _CONV_HALO, :]

    ext_ref[_CONV_HALO:_CONV_HALO + ts, :] = u_ref[...].astype(F32)
    cw = 512
    for c0 in range(0, SSD_CONV_CH, cw):
        ext = ext_ref[:, c0:c0 + cw]
        acc = b_ref[:, c0:c0 + cw] + w_ref[SSD_CONV - 1, 0:1, c0:c0 + cw] * ext[_CONV_HALO:]
        for d in range(1, SSD_CONV):
            rolled = pltpu.roll(ext, d, axis=0)
            acc = acc + w_ref[SSD_CONV - 1 - d, 0:1, c0:c0 + cw] * rolled[_CONV_HALO:]
        o_ref[:, c0:c0 + cw] = acc * jax.nn.sigmoid(acc)


def _conv_silu(xbc, conv_w, conv_b, bsz, seq, ts=256):
    T = bsz * seq
    ns = seq // ts
    conv_w = jnp.broadcast_to(conv_w[:, None, :], (SSD_CONV, 8, SSD_CONV_CH))
    return pl.pallas_call(
        _conv_kernel,
        grid=(bsz, ns),
        in_specs=[
            pl.BlockSpec((ts, SSD_CONV_CH), lambda b, s: (b * ns + s, 0)),
            pl.BlockSpec((SSD_CONV, 8, SSD_CONV_CH), lambda b, s: (0, 0, 0)),
            pl.BlockSpec((1, SSD_CONV_CH), lambda b, s: (0, 0)),
        ],
        out_specs=pl.BlockSpec((ts, SSD_CONV_CH), lambda b, s: (b * ns + s, 0)),
        out_shape=jax.ShapeDtypeStruct((T, SSD_CONV_CH), F32),
        scratch_shapes=[pltpu.VMEM((ts + _CONV_HALO, SSD_CONV_CH), F32)],
        compiler_params=_cparams("parallel", "arbitrary"),
        name="conv_silu",
    )(xbc, conv_w, conv_b)


def _ssd_kernel(xs_ref, bm_ref, cm_ref, z_ref, dtr_ref, dtb_ref, a128_ref, dexp_ref, ng_ref, e_ref,
                o_ref, state_ref, y_ref):
    L = SSD_CHUNK
    c = pl.program_id(1)

    @pl.when(c == 0)
    def _():
        state_ref[...] = jnp.zeros_like(state_ref)

    row = lax.broadcasted_iota(jnp.int32, (L, L), 0)
    col = lax.broadcasted_iota(jnp.int32, (L, L), 1)
    incl = (col <= row).astype(F32)
    eye = (col == row).astype(F32)
    lower = col <= row
    lane = lax.broadcasted_iota(jnp.int32, (L, LANES), 1)

    def expand(v):
        hi = v.astype(BF16)
        r1 = v - hi.astype(F32)
        mid = r1.astype(BF16)
        lo = (r1 - mid.astype(F32)).astype(BF16)
        return _dot(jnp.concatenate([hi, mid, lo], axis=1), e_ref[...])

    dt = jax.nn.softplus(dtr_ref[...] + dtb_ref[...])
    dt_exp = expand(dt)
    cs_col = _dot(incl, dt * a128_ref[...], HIGHEST)
    cs_exp = expand(cs_col)
    cs_row = _dot_nt(eye, cs_col, HIGHEST)

    xs = xs_ref[...]
    xdt = xs * dt_exp
    xdt_b = xdt.astype(BF16)
    cs_last = cs_exp[L - 1:L, :]
    xw_b = (xdt * jnp.exp(cs_last - cs_exp)).astype(BF16)
    grow = jnp.exp(cs_exp)
    grow_last = jnp.exp(cs_last)

    gw = SSD_GROUP_WIDTH
    heads_per_group = SSD_HEADS // SSD_GROUPS
    for g in range(SSD_GROUPS):
        bg = bm_ref[:, g * SSD_STATE:(g + 1) * SSD_STATE]
        cg = cm_ref[:, g * SSD_STATE:(g + 1) * SSD_STATE].astype(BF16)
        cb = _dot_nt(cg, bg.astype(BF16))
        for pair in range(heads_per_group // 2):
            c0 = g * gw + pair * LANES
            x_pair = xdt_b[:, c0:c0 + LANES]
            ys = []
            for sub in range(2):
                hd = g * heads_per_group + pair * 2 + sub
                diff = cs_col[:, hd:hd + 1] - cs_row[hd:hd + 1, :]
                seg = jnp.where(lower, jnp.exp(jnp.minimum(diff, 0.0)), 0.0)
                ys.append(_dot((cb * seg).astype(BF16), x_pair))
            y_ref[:, c0:c0 + LANES] = jnp.where(lane < SSD_HEAD_DIM, ys[0], ys[1])
        st = state_ref[g]
        y_ref[:, g * gw:(g + 1) * gw] += _dot(cg, st.astype(BF16)) * grow[:, g * gw:(g + 1) * gw]
        state_ref[g] = st * grow_last[:, g * gw:(g + 1) * gw] + _dot(bg.T.astype(BF16), xw_b[:, g * gw:(g + 1) * gw])

    z = z_ref[...].astype(F32)
    y = (y_ref[...] + dexp_ref[...] * xs) * (z * jax.nn.sigmoid(z))
    for g in range(SSD_GROUPS):
        yg = y[:, g * gw:(g + 1) * gw]
        ms = jnp.mean(yg * yg, axis=-1, keepdims=True)
        o_ref[:, g * gw:(g + 1) * gw] = (yg * lax.rsqrt(ms + LN_EPS) * ng_ref[:, g * gw:(g + 1) * gw]).astype(BF16)


def _ssd(xbc_act, z, dt_raw, dt_bias128, a128, d_exp, norm_g, bsz, seq):
    L = SSD_CHUNK
    T = bsz * seq
    nc = seq // L
    nb = SSD_INNER // SSD_GROUP_WIDTH
    head_of_row = jnp.arange(3 * LANES) % LANES
    expand = (head_of_row[:, None] == (jnp.arange(SSD_INNER)[None, :] // SSD_HEAD_DIM)).astype(BF16)
    full = lambda shape: pl.BlockSpec(shape, lambda b, c: (0,) * len(shape))
    return pl.pallas_call(
        _ssd_kernel,
        grid=(bsz, nc),
        in_specs=[
            pl.BlockSpec((L, SSD_INNER), lambda b, c: (b * nc + c, 0)),
            pl.BlockSpec((L, SSD_GROUPS * SSD_STATE), lambda b, c: (b * nc + c, nb)),
            pl.BlockSpec((L, SSD_GROUPS * SSD_STATE), lambda b, c: (b * nc + c, nb + 1)),
            pl.BlockSpec((L, SSD_INNER), lambda b, c: (b * nc + c, 0)),
            pl.BlockSpec((L, LANES), lambda b, c: (b * nc + c, 0)),
            full((1, LANES)), full((1, LANES)), full((1, SSD_INNER)), full((1, SSD_INNER)),
            full((3 * LANES, SSD_INNER)),
        ],
        out_specs=pl.BlockSpec((L, SSD_INNER), lambda b, c: (b * nc + c, 0)),
        out_shape=jax.ShapeDtypeStruct((T, SSD_INNER), BF16),
        scratch_shapes=[
            pltpu.VMEM((SSD_GROUPS, SSD_STATE, SSD_GROUP_WIDTH), F32),
            pltpu.VMEM((L, SSD_INNER), F32),
        ],
        compiler_params=_cparams("parallel", "arbitrary"),
        name="ssd_scan",
    )(xbc_act, xbc_act, xbc_act, z, dt_raw, dt_bias128, a128, d_exp, norm_g, expand)


def _merge_kernel(osb_ref, ossd_ref, gl_ref, h_ref, bg_ref, wsb_ref, wssd_ref, wmix_ref, g_ref, b_ref, o_ref):
    gates = jax.nn.sigmoid(gl_ref[...].astype(F32) + bg_ref[...])
    merged = (gates[:, :D_MODEL] * _dot(osb_ref[...], wsb_ref[...])
              + gates[:, D_MODEL:] * _dot(ossd_ref[...], wssd_ref[...]))
    mix = _dot(merged.astype(BF16), wmix_ref[...])
    o_ref[...] = _layer_norm(DN_ALPHA * h_ref[...] + mix, g_ref[...], b_ref[...])


def _merge(o_sb, o_ssd, gate_logits, h, b_gate, w_sb, w_ssd, w_mix, ln_g, ln_b, tm=256):
    T = h.shape[0]
    full = lambda shape: pl.BlockSpec(shape, lambda i: (0,) * len(shape))
    return pl.pallas_call(
        _merge_kernel,
        grid=(T // tm,),
        in_specs=[
            pl.BlockSpec((tm, D_MODEL), lambda i: (i, 0)),
            pl.BlockSpec((tm, SSD_INNER), lambda i: (i, 0)),
            pl.BlockSpec((tm, 2 * D_MODEL), lambda i: (i, 0)),
            pl.BlockSpec((tm, D_MODEL), lambda i: (i, 0)),
            full((1, 2 * D_MODEL)), full((D_MODEL, D_MODEL)), full((SSD_INNER, D_MODEL)), full((D_MODEL, D_MODEL)),
            full((1, D_MODEL)), full((1, D_MODEL)),
        ],
        out_specs=pl.BlockSpec((tm, D_MODEL), lambda i: (i, 0)),
        out_shape=jax.ShapeDtypeStruct((T, D_MODEL), F32),
        compiler_params=_cparams("parallel"),
        name="merge",
    )(o_sb, o_ssd, gate_logits, h, b_gate, w_sb, w_ssd, w_mix, ln_g, ln_b)


def _kv_kernel(m_ref, wk_ref, wv_ref, k_ref, v_ref):
    mb = m_ref[...].astype(BF16)
    k_ref[...] = _dot(mb, wk_ref[...]).astype(BF16)
    v_ref[...] = _dot(mb, wv_ref[...]).astype(BF16)


def _kv_proj(mem2, w_xk, w_xv, tm=512):
    M = mem2.shape[0]
    full = lambda shape: pl.BlockSpec(shape, lambda i: (0,) * len(shape))
    return pl.pallas_call(
        _kv_kernel,
        grid=(M // tm,),
        in_specs=[pl.BlockSpec((tm, D_MODEL), lambda i: (i, 0)), full((D_MODEL, D_MODEL)), full((D_MODEL, D_MODEL))],
        out_specs=[pl.BlockSpec((tm, D_MODEL), lambda i: (i, 0))] * 2,
        out_shape=[jax.ShapeDtypeStruct((M, D_MODEL), BF16)] * 2,
        compiler_params=_cparams("parallel"),
        name="kv_proj",
    )(mem2, w_xk, w_xv)


def _pack_bf16_pairs(x):
    bits = lax.bitcast_convert_type(x.astype(BF16).astype(F32), jnp.uint32)
    half = x.shape[1] // 2
    return (bits[:, :half] >> 16) | (bits[:, half:] & jnp.uint32(0xFFFF0000))


def _unpack_bf16_pairs(u):
    lo = lax.bitcast_convert_type(u << 16, F32)
    hi = lax.bitcast_convert_type(u & jnp.uint32(0xFFFF0000), F32)
    return jnp.concatenate([lo, hi], axis=1).astype(BF16)


def _xattn_kernel(h_ref, k_ref, v_ref, wq_ref, wo_ref, g_ref, b_ref, o_ref, op_ref):
    h = h_ref[...]
    q = (_dot(h.astype(BF16), wq_ref[...]) * (X_HEAD_DIM ** -0.5)).astype(BF16)
    outs = []
    for hd in range(X_HEADS):
        sl = slice(hd * X_HEAD_DIM, (hd + 1) * X_HEAD_DIM)
        s = _dot_nt(q[:, sl], k_ref[:, sl])
        e = jnp.exp(s - jnp.max(s, axis=-1, keepdims=True))
        p = e / jnp.sum(e, axis=-1, keepdims=True)
        outs.append(_dot(p.astype(BF16), v_ref[:, sl]))
    o = jnp.concatenate(outs, axis=-1).astype(BF16)
    h2 = _layer_norm(DN_ALPHA * h + _dot(o, wo_ref[...]), g_ref[...], b_ref[...])
    o_ref[...] = h2
    op_ref[...] = _pack_bf16_pairs(h2)


def _xattn(h1, k, v, w_xq, w_xo, ln_g, ln_b, bsz, seq, tm=256):
    T = bsz * seq
    nt = seq // tm
    full = lambda shape: pl.BlockSpec(shape, lambda b, i: (0,) * len(shape))
    return pl.pallas_call(
        _xattn_kernel,
        grid=(bsz, nt),
        in_specs=[
            pl.BlockSpec((tm, D_MODEL), lambda b, i: (b * nt + i, 0)),
            pl.BlockSpec((MEM_LEN, D_MODEL), lambda b, i: (b, 0)),
            pl.BlockSpec((MEM_LEN, D_MODEL), lambda b, i: (b, 0)),
            full((D_MODEL, D_MODEL)), full((D_MODEL, D_MODEL)), full((1, D_MODEL)), full((1, D_MODEL)),
        ],
        out_specs=[pl.BlockSpec((tm, D_MODEL), lambda b, i: (b * nt + i, 0)),
                   pl.BlockSpec((tm, D_MODEL // 2), lambda b, i: (b * nt + i, 0))],
        out_shape=[jax.ShapeDtypeStruct((T, D_MODEL), F32), jax.ShapeDtypeStruct((T, D_MODEL // 2), jnp.uint32)],
        compiler_params=_cparams("parallel", "parallel"),
        name="xattn",
    )(h1, k, v, w_xq, w_xo, ln_g, ln_b)


def _router_kernel(h_ref, w_ref, b_ref, idx_ref, p_ref, rank_ref, cnt_ref, run_ref):
    tm = h_ref.shape[0]
    i = pl.program_id(0)

    @pl.when(i == 0)
    def _():
        run_ref[...] = jnp.zeros_like(run_ref)

    lane = lax.broadcasted_iota(jnp.int32, (tm, LANES), 1)
    lane_f = lane.astype(F32)
    neg = jnp.float32(-jnp.inf)
    logits = _dot(h_ref[...], w_ref[...], HIGHEST) + b_ref[...]
    work = jnp.where(lane < N_EXPERTS, logits, neg)
    vals, hots = [], []
    idx_out = jnp.zeros((tm, LANES), F32)
    for k in range(TOP_K):
        m = jnp.max(work, axis=-1, keepdims=True)
        first = jnp.min(jnp.where(work == m, lane_f, float(LANES)), axis=-1, keepdims=True)
        hot = lane_f == first
        vals.append(m)
        hots.append(hot)
        idx_out = jnp.where(lane == k, first, idx_out)
        work = jnp.where(hot, neg, work)
    exps = [jnp.exp(v - vals[0]) for v in vals]
    denom = exps[0] + exps[1] + exps[2] + exps[3]
    member = jnp.zeros((tm, LANES), F32)
    p_out = jnp.zeros((tm, LANES), F32)
    for k in range(TOP_K):
        member = jnp.where(hots[k], 1.0, member)
        p_out = jnp.where(lane == k, exps[k] / denom, p_out)
    row = lax.broadcasted_iota(jnp.int32, (tm, tm), 0)
    col = lax.broadcasted_iota(jnp.int32, (tm, tm), 1)
    before = (col < row).astype(BF16)
    earlier = _dot(before, member.astype(BF16)) + run_ref[...]
    rank_out = jnp.zeros((tm, LANES), F32)
    for k in range(TOP_K):
        r = jnp.sum(jnp.where(hots[k], earlier, 0.0), axis=-1, keepdims=True)
        rank_out = jnp.where(lane == k, r, rank_out)
    run = run_ref[...] + jnp.sum(member, axis=0, keepdims=True)
    run_ref[...] = run
    cnt_ref[...] = run
    idx_ref[...] = idx_out.astype(jnp.int32)
    p_ref[...] = p_out
    rank_ref[...] = rank_out.astype(jnp.int32)


def _router(h2, w_r, b_r, tm=512):
    T = h2.shape[0]
    full = lambda shape: pl.BlockSpec(shape, lambda i: (0,) * len(shape))
    tok = pl.BlockSpec((tm, LANES), lambda i: (i, 0))
    return pl.pallas_call(
        _router_kernel,
        grid=(T // tm,),
        in_specs=[pl.BlockSpec((tm, D_MODEL), lambda i: (i, 0)), full((D_MODEL, LANES)), full((1, LANES))],
        out_specs=[tok, tok, tok, full((1, LANES))],
        out_shape=[
            jax.ShapeDtypeStruct((T, LANES), jnp.int32),
            jax.ShapeDtypeStruct((T, LANES), F32),
            jax.ShapeDtypeStruct((T, LANES), jnp.int32),
            jax.ShapeDtypeStruct((1, LANES), F32),
        ],
        scratch_shapes=[pltpu.VMEM((1, LANES), F32)],
        compiler_params=_cparams("arbitrary"),
        name="router",
    )(h2, w_r, b_r)


def _dispatch_kernel(dest_ref, h_ref, xin_ref, x_ref, sem):
    del xin_ref
    td = h_ref.shape[0]

    def issue(t, _):
        for k in range(TOP_K):
            d = dest_ref[t * TOP_K + k]
            pltpu.make_async_copy(h_ref.at[pl.ds(t, 1), :], x_ref.at[pl.ds(d, 1), :], sem).start(priority=k % 2)
        return 0

    lax.fori_loop(0, td, issue, 0)

    def wait(t, _):
        for k in range(TOP_K):
            pltpu.make_async_copy(h_ref.at[pl.ds(0, 1), :], x_ref.at[pl.ds(0, 1), :], sem).wait()
        return 0

    lax.fori_loop(0, td, wait, 0, unroll=8)


def _dispatch(dest_flat, h2_packed, n_rows, td=256):
    T, width = h2_packed.shape
    x0 = jnp.zeros((n_rows, width), h2_packed.dtype)
    return pl.pallas_call(
        _dispatch_kernel,
        grid=(T // td,),
        in_specs=[
            pl.BlockSpec((td * TOP_K,), lambda i: (i,), memory_space=pltpu.SMEM),
            pl.BlockSpec((td, width), lambda i: (i, 0)),
            pl.BlockSpec(memory_space=pl.ANY),
        ],
        out_specs=pl.BlockSpec(memory_space=pl.ANY),
        out_shape=jax.ShapeDtypeStruct((n_rows, width), h2_packed.dtype),
        scratch_shapes=[pltpu.SemaphoreType.DMA(())],
        input_output_aliases={2: 0},
        compiler_params=_cparams("arbitrary"),
        name="moe_dispatch",
    )(dest_flat, h2_packed, x0)


def _expert_kernel(be_ref, nused_ref, x_ref, wg_ref, bg_ref, wu_ref, bu_ref, wd_ref, bd_ref, y_ref,
                   wg_bf, wu_bf, wd_bf):
    blk = pl.program_id(0)
    new_expert = (blk == 0) | (be_ref[blk] != be_ref[jnp.maximum(blk - 1, 0)])

    @pl.when(new_expert & (blk < nused_ref[0]))
    def _():
        wg_bf[...] = wg_ref[0].astype(BF16)
        wu_bf[...] = wu_ref[0].astype(BF16)
        wd_bf[...] = wd_ref[0].astype(BF16)

    @pl.when(blk < nused_ref[0])
    def _():
        xb = _unpack_bf16_pairs(x_ref[...])
        g = jnp.minimum(_dot(xb, wg_bf[...]) + bg_ref[0], SWIGLU_LIMIT)
        u = jnp.clip(_dot(xb, wu_bf[...]) + bu_ref[0], -SWIGLU_LIMIT, SWIGLU_LIMIT)
        act = (u + 1.0) * g * jax.nn.sigmoid(SWIGLU_ALPHA * g)
        y_ref[...] = _dot(act.astype(BF16), wd_bf[...]) + bd_ref[0]

    @pl.when(blk >= nused_ref[0])
    def _():
        y_ref[...] = jnp.zeros_like(y_ref)


def _experts(block_e, n_used, x_rows, w_gate, b_gate, w_up, b_up, w_down, b_down):
    n_rows = x_rows.shape[0]
    bm = EXPERT_ROWS
    wspec = pl.BlockSpec((1, D_MODEL, D_MODEL), lambda i, be, nu: (be[i], 0, 0))
    bspec = pl.BlockSpec((1, 1, D_MODEL), lambda i, be, nu: (be[i], 0, 0))
    return pl.pallas_call(
        _expert_kernel,
        grid_spec=pltpu.PrefetchScalarGridSpec(
            num_scalar_prefetch=2,
            grid=(n_rows // bm,),
            in_specs=[pl.BlockSpec((bm, D_MODEL // 2), lambda i, be, nu: (i, 0)),
                      wspec, bspec, wspec, bspec, wspec, bspec],
            out_specs=pl.BlockSpec((bm, D_MODEL), lambda i, be, nu: (i, 0)),
            scratch_shapes=[pltpu.VMEM((D_MODEL, D_MODEL), BF16)] * 3,
        ),
        out_shape=jax.ShapeDtypeStruct((n_rows, D_MODEL), F32),
        compiler_params=_cparams("arbitrary"),
        name="moe_experts",
    )(block_e, n_used, x_rows, w_gate, b_gate, w_up, b_up, w_down, b_down)


def _combine_kernel(dcur_ref, dnext_ref, y_ref, p_ref, h_ref, g_ref, b_ref, o_ref, buf, sem):
    tc = h_ref.shape[0]
    i = pl.program_id(0)
    n = pl.num_programs(0)

    def issue(dref, slot):
        def body(t, _):
            for k in range(TOP_K):
                d = dref[t * TOP_K + k]
                pltpu.make_async_copy(y_ref.at[pl.ds(d, 1), :], buf.at[slot, k, pl.ds(t, 1), :],
                                      sem.at[slot]).start(priority=k % 2)
            return 0
        lax.fori_loop(0, tc, body, 0, unroll=4)

    slot = i % 2

    @pl.when(i == 0)
    def _():
        issue(dcur_ref, 0)

    @pl.when(i + 1 < n)
    def _():
        issue(dnext_ref, 1 - slot)

    def wait_body(t, _):
        for k in range(TOP_K):
            pltpu.make_async_copy(y_ref.at[pl.ds(0, 1), :], buf.at[slot, k, pl.ds(0, 1), :], sem.at[slot]).wait()
        return 0

    lax.fori_loop(0, tc, wait_body, 0, unroll=8)

    p = p_ref[...]
    y = p[:, 0:1] * buf[slot, 0]
    for k in range(1, TOP_K):
        y = y + p[:, k:k + 1] * buf[slot, k]
    o_ref[...] = _layer_norm(DN_ALPHA * h_ref[...] + y, g_ref[...], b_ref[...])


def _combine(dest_flat, y_rows, probs, h2, ln_g, ln_b, tc=128):
    T = h2.shape[0]
    nt = T // tc
    full = lambda shape: pl.BlockSpec(shape, lambda i: (0,) * len(shape))
    return pl.pallas_call(
        _combine_kernel,
        grid=(nt,),
        in_specs=[
            pl.BlockSpec((tc * TOP_K,), lambda i: (i,), memory_space=pltpu.SMEM),
            pl.BlockSpec((tc * TOP_K,), lambda i: (jnp.minimum(i + 1, nt - 1),), memory_space=pltpu.SMEM),
            pl.BlockSpec(memory_space=pl.ANY),
            pl.BlockSpec((tc, LANES), lambda i: (i, 0)),
            pl.BlockSpec((tc, D_MODEL), lambda i: (i, 0)),
            full((1, D_MODEL)), full((1, D_MODEL)),
        ],
        out_specs=pl.BlockSpec((tc, D_MODEL), lambda i: (i, 0)),
        out_shape=jax.ShapeDtypeStruct((T, D_MODEL), F32),
        scratch_shapes=[pltpu.VMEM((2, TOP_K, tc, D_MODEL), F32), pltpu.SemaphoreType.DMA((2,))],
        compiler_params=_cparams("arbitrary"),
        name="moe_combine",
    )(dest_flat, dest_flat, y_rows, probs, h2, ln_g, ln_b)


def _pad_lanes(v, fill=0.0):
    v = v.reshape(1, -1)
    return jnp.pad(v, ((0, 0), (0, LANES - v.shape[1])), constant_values=fill)


def _mixer_stage(x2, bsz, seq, ln_in_g, ln_in_b, w_in, b_branch_gate, conv_w, conv_b, dt_bias, a_log, d_skip,
                 ssd_norm_g, w_sb, w_ssd, w_mix_out, ln1_g, ln1_b):
    n_lin = 3 * D_MODEL + SSD_INNER + SSD_CONV_CH
    w_main = jnp.concatenate([w_in[:, :n_lin], w_in[:, n_lin + SSD_HEADS:]], axis=1).astype(BF16)
    w_dt = jnp.pad(w_in[:, n_lin:n_lin + SSD_HEADS], ((0, 0), (0, LANES - SSD_HEADS))).astype(BF16)
    h, qkv, z, xbc, gate_logits, dt_raw = _inproj(x2, ln_in_g.reshape(1, -1), ln_in_b.reshape(1, -1), w_main, w_dt)

    o_sb = _stick_breaking(qkv, bsz, seq)

    xbc_act = _conv_silu(xbc, conv_w, conv_b.reshape(1, -1), bsz, seq)
    a = -jnp.exp(a_log.astype(F32))
    o_ssd = _ssd(xbc_act, z, dt_raw, _pad_lanes(dt_bias), _pad_lanes(a),
                 jnp.repeat(d_skip, SSD_HEAD_DIM).reshape(1, -1), ssd_norm_g.reshape(1, -1), bsz, seq)

    return _merge(o_sb, o_ssd, gate_logits, h, b_branch_gate.reshape(1, -1), w_sb.astype(BF16), w_ssd.astype(BF16),
                  w_mix_out.astype(BF16), ln1_g.reshape(1, -1), ln1_b.reshape(1, -1))


def _xattn_stage(h1, mem2, bsz, seq, w_xq, w_xk, w_xv, w_xo, ln2_g, ln2_b):
    k, v = _kv_proj(mem2, w_xk.astype(BF16), w_xv.astype(BF16))
    return _xattn(h1, k, v, w_xq.astype(BF16), w_xo.astype(BF16), ln2_g.reshape(1, -1), ln2_b.reshape(1, -1),
                  bsz, seq)


def _moe_stage(h2, h2_packed, w_router, b_router, w_e_gate, b_e_gate, w_e_up, b_e_up, w_e_down, b_e_down, ln3_g,
               ln3_b):
    T = h2.shape[0]
    bm = EXPERT_ROWS
    w_r = jnp.pad(w_router, ((0, 0), (0, LANES - N_EXPERTS)))
    idx_p, probs, rank_p, counts_p = _router(h2, w_r, _pad_lanes(b_router))
    idx = idx_p[:, :TOP_K]
    rank = rank_p[:, :TOP_K]
    counts = counts_p[0, :N_EXPERTS].astype(jnp.int32)
    padded = (counts + bm - 1) // bm * bm
    end_padded = jnp.cumsum(padded)
    start_padded = end_padded - padded
    onehot = idx[:, :, None] == jnp.arange(N_EXPERTS, dtype=jnp.int32)[None, None, :]
    dest = jnp.sum(jnp.where(onehot, start_padded[None, None, :], 0), axis=-1) + rank
    dest_flat = dest.reshape(-1).astype(jnp.int32)
    n_blocks = -(-(T * TOP_K + N_EXPERTS * (bm - 1)) // bm)
    n_rows = n_blocks * bm
    blk_start = jnp.arange(n_blocks, dtype=jnp.int32) * bm
    block_e = jnp.minimum(jnp.sum(blk_start[:, None] >= end_padded[None, :], axis=-1), N_EXPERTS - 1).astype(jnp.int32)
    n_used = (end_padded[-1:] // bm).astype(jnp.int32)

    x_rows = _dispatch(dest_flat, h2_packed, n_rows)
    y_rows = _experts(block_e, n_used, x_rows, w_e_gate, b_e_gate[:, None, :], w_e_up, b_e_up[:, None, :],
                      w_e_down, b_e_down[:, None, :])
    return _combine(dest_flat, y_rows, probs, h2, ln3_g.reshape(1, -1), ln3_b.reshape(1, -1))


def kernel(x, mem, ln_in_g, ln_in_b, w_in, b_branch_gate, conv_w, conv_b, dt_bias, a_log, d_skip, ssd_norm_g, w_sb,
           w_ssd, w_mix_out, ln1_g, ln1_b, w_xq, w_xk, w_xv, w_xo, ln2_g, ln2_b, w_router, b_router, w_e_gate,
           b_e_gate, w_e_up, b_e_up, w_e_down, b_e_down, ln3_g, ln3_b):
    bsz, seq, _ = x.shape
    depth = w_in.shape[0]
    x2 = x.reshape(bsz * seq, D_MODEL)
    mem2 = mem.reshape(bsz * mem.shape[1], D_MODEL)
    assert depth == 1, "the entry LayerNorm is fused into the single layer's input projection"
    l = 0
    h1 = _mixer_stage(x2, bsz, seq, ln_in_g, ln_in_b, w_in[l], b_branch_gate[l], conv_w[l], conv_b[l], dt_bias[l],
                      a_log[l], d_skip[l], ssd_norm_g[l], w_sb[l], w_ssd[l], w_mix_out[l], ln1_g[l], ln1_b[l])
    h2, h2_packed = _xattn_stage(h1, mem2, bsz, seq, w_xq[l], w_xk[l], w_xv[l], w_xo[l], ln2_g[l], ln2_b[l])
    h = _moe_stage(h2, h2_packed, w_router[l], b_router[l], w_e_gate[l], b_e_gate[l], w_e_up[l], b_e_up[l], w_e_down[l],
                   b_e_down[l], ln3_g[l], ln3_b[l])
    return h.reshape(bsz, seq, D_MODEL)
```

```python
import functools

import jax
import jax.numpy as jnp
from jax import lax
from jax.experimental import pallas as pl
from jax.experimental.pallas import tpu as pltpu

F32 = jnp.float32
BF16 = jnp.bfloat16
HIGHEST = lax.Precision.HIGHEST

D_MODEL = 1024
LN_EPS = 1e-5
DN_ALPHA = 2.0 ** 0.25
LANES = 128

SB_HEADS = 16
SB_HEAD_DIM = 64
SB_TILE = 256
SB_LANES = 256
SB_GROUP = SB_LANES // SB_HEAD_DIM
LOG2E = 1.4426950408889634
SB_SKIP_BITS = 152.0

SSD_INNER = 2048
SSD_HEADS = 32
SSD_HEAD_DIM = 64
SSD_GROUPS = 4
SSD_GROUP_WIDTH = SSD_INNER // SSD_GROUPS
SSD_STATE = 128
SSD_CONV = 4
SSD_CONV_CH = SSD_INNER + 2 * SSD_GROUPS * SSD_STATE
SSD_CHUNK = 128

X_HEADS = 4
X_HEAD_DIM = 256
MEM_LEN = 256

N_EXPERTS = 32
TOP_K = 4
SWIGLU_LIMIT = 7.0
SWIGLU_ALPHA = 1.702
EXPERT_ROWS = 512

VMEM_LIMIT = 52 * 1024 * 1024


def _cparams(*sem):
    return pltpu.CompilerParams(dimension_semantics=sem, vmem_limit_bytes=VMEM_LIMIT)


def _layer_norm(x, g, b):
    mu = jnp.mean(x, axis=-1, keepdims=True)
    xc = x - mu
    var = jnp.mean(xc * xc, axis=-1, keepdims=True)
    return xc * lax.rsqrt(var + LN_EPS) * g + b


def _dot(a, b, precision=None):
    return jnp.dot(a, b, preferred_element_type=F32, precision=precision)


def _dot_nt(a, b, precision=None):
    return lax.dot_general(a, b, (((1,), (1,)), ((), ())), preferred_element_type=F32, precision=precision)


_IN_TN = 1024
_N_QKV, _N_Z, _N_XBC, _N_GATE = (3 * D_MODEL // _IN_TN, SSD_INNER // _IN_TN, SSD_CONV_CH // _IN_TN,
                                 2 * D_MODEL // _IN_TN)
_QKV_SPLIT = D_MODEL // _IN_TN


def _inproj_kernel(x_ref, g_ref, b_ref, w_ref, wdt_ref, h_ref, qkv_ref, z_ref, xbc_ref, gate_ref, dt_ref, hb_ref):
    j = pl.program_id(1)

    @pl.when(j == 0)
    def _():
        h = _layer_norm(x_ref[...], g_ref[...], b_ref[...])
        h_ref[...] = h
        hb = h.astype(BF16)
        hb_ref[...] = hb
        dt_ref[...] = _dot(hb, wdt_ref[...])

    @pl.when(j < _N_QKV)
    def _():
        for c in range(_IN_TN // SB_LANES):
            sl = slice(c * SB_LANES, (c + 1) * SB_LANES)
            qkv_ref[0, c] = _dot(hb_ref[...], w_ref[:, sl]).astype(BF16)

    @pl.when((j >= _N_QKV) & (j < _N_QKV + _N_Z))
    def _():
        z_ref[...] = _dot(hb_ref[...], w_ref[...]).astype(BF16)

    @pl.when((j >= _N_QKV + _N_Z) & (j < _N_QKV + _N_Z + _N_XBC))
    def _():
        xbc_ref[...] = _dot(hb_ref[...], w_ref[...]).astype(BF16)

    @pl.when(j >= _N_QKV + _N_Z + _N_XBC)
    def _():
        gate_ref[...] = _dot(hb_ref[...], w_ref[...]).astype(BF16)


def _inproj(x2, ln_g, ln_b, w_main, w_dt, tm=1024):
    T = x2.shape[0]
    tm = min(tm, T)
    tn = _IN_TN
    nj = _N_QKV + _N_Z + _N_XBC + _N_GATE
    o_z, o_xbc, o_gate = _N_QKV, _N_QKV + _N_Z, _N_QKV + _N_Z + _N_XBC

    def qkv_map(i, j):
        jq = jnp.minimum(j, _N_QKV - 1)
        return (jq // _QKV_SPLIT, jq % _QKV_SPLIT, i, 0)

    return pl.pallas_call(
        _inproj_kernel,
        grid=(T // tm, nj),
        in_specs=[
            pl.BlockSpec((tm, D_MODEL), lambda i, j: (i, 0)),
            pl.BlockSpec((1, D_MODEL), lambda i, j: (0, 0)),
            pl.BlockSpec((1, D_MODEL), lambda i, j: (0, 0)),
            pl.BlockSpec((D_MODEL, tn), lambda i, j: (0, j)),
            pl.BlockSpec((D_MODEL, LANES), lambda i, j: (0, 0)),
        ],
        out_specs=[
            pl.BlockSpec((tm, D_MODEL), lambda i, j: (i, 0)),
            pl.BlockSpec((1, tn // SB_LANES, tm, SB_LANES), qkv_map),
            pl.BlockSpec((tm, tn), lambda i, j: (i, jnp.clip(j - o_z, 0, _N_Z - 1))),
            pl.BlockSpec((tm, tn), lambda i, j: (i, jnp.clip(j - o_xbc, 0, _N_XBC - 1))),
            pl.BlockSpec((tm, tn), lambda i, j: (i, jnp.clip(j - o_gate, 0, _N_GATE - 1))),
            pl.BlockSpec((tm, LANES), lambda i, j: (i, 0)),
        ],
        out_shape=[
            jax.ShapeDtypeStruct((T, D_MODEL), F32),
            jax.ShapeDtypeStruct((3, D_MODEL // SB_LANES, T, SB_LANES), BF16),
            jax.ShapeDtypeStruct((T, SSD_INNER), BF16),
            jax.ShapeDtypeStruct((T, SSD_CONV_CH), BF16),
            jax.ShapeDtypeStruct((T, 2 * D_MODEL), BF16),
            jax.ShapeDtypeStruct((T, LANES), F32),
        ],
        scratch_shapes=[pltpu.VMEM((tm, D_MODEL), BF16)],
        compiler_params=_cparams("parallel", "arbitrary"),
        name="inproj",
    )(x2, ln_g, ln_b, w_main, w_dt)


def _sb_kernel(q_ref, k_ref, v_ref, o_ref):
    t = SB_TILE
    G = SB_GROUP
    i = pl.program_id(2)
    lane = lax.broadcasted_iota(jnp.int32, (t, SB_LANES), 1)
    head_lanes = [(lane >= h * SB_HEAD_DIM) & (lane < (h + 1) * SB_HEAD_DIM) for h in range(G)]
    qs = q_ref[0, 0] * BF16(SB_HEAD_DIM ** -0.5)
    zero = jnp.zeros_like(qs)
    q_stack = jnp.concatenate([jnp.where(m, qs, zero) for m in head_lanes], axis=0)
    row = lax.broadcasted_iota(jnp.int32, (G * t, t), 0) & (t - 1)
    col = lax.broadcasted_iota(jnp.int32, (G * t, t), 1)
    causal = col < row
    krow = lax.broadcasted_iota(jnp.int32, (2 * t, t), 0) & (t - 1)
    kcol = lax.broadcasted_iota(jnp.int32, (2 * t, t), 1)
    later2 = (krow > kcol).astype(BF16)

    def block(jb, c, acc, diag):
        k = k_ref[0, 0, pl.ds(pl.multiple_of(jb * t, t), t), :]
        v = v_ref[0, 0, pl.ds(pl.multiple_of(jb * t, t), t), :]
        vzero = jnp.zeros_like(v)
        v_stack = jnp.concatenate([jnp.where(m, v, vzero) for m in head_lanes], axis=0)
        z2 = _dot_nt(q_stack, k) * LOG2E
        neg_abs = lax.bitcast_convert_type(lax.bitcast_convert_type(z2, jnp.uint32) | jnp.uint32(0x80000000), F32)
        sp2 = jnp.maximum(z2, 0.0) + jnp.log2(1.0 + jnp.exp2(neg_abs))
        spm = jnp.where(causal, sp2, 0.0) if diag else sp2
        hi = spm.astype(BF16)
        lo = (spm - hi.astype(F32)).astype(BF16)
        after = _dot(jnp.concatenate([hi, lo], axis=1), later2)
        w = jnp.exp2(z2 - sp2 - after - c)
        if diag:
            w = jnp.where(causal, w, 0.0)
        wb = w.astype(BF16)
        w_cat = jnp.concatenate([wb[h * t:(h + 1) * t] for h in range(G)], axis=1)
        return c + jnp.sum(spm, axis=1, keepdims=True), acc + _dot(w_cat, v_stack)

    c, acc = block(i, jnp.zeros((G * t, 1), F32), jnp.zeros((t, SB_LANES), F32), True)

    def cond(carry):
        jb, cmin, _, _ = carry
        return (jb >= 0) & (cmin < SB_SKIP_BITS)

    def body(carry):
        jb, _, c, acc = carry
        c, acc = block(jb, c, acc, False)
        return jb - 1, jnp.min(c), c, acc

    _, _, _, acc = lax.while_loop(cond, body, (i - 1, jnp.min(c), c, acc))
    o_ref[...] = acc.astype(BF16)


def _stick_breaking(qkv, bsz, seq):
    t = SB_TILE
    nq = seq // t
    T = bsz * seq
    return pl.pallas_call(
        _sb_kernel,
        grid=(bsz, SB_HEADS // SB_GROUP, nq),
        in_specs=[
            pl.BlockSpec((1, 1, t, SB_LANES), lambda b, c, i: (0, c, b * nq + i, 0)),
            pl.BlockSpec((1, 1, seq, SB_LANES), lambda b, c, i: (1, c, b, 0)),
            pl.BlockSpec((1, 1, seq, SB_LANES), lambda b, c, i: (2, c, b, 0)),
        ],
        out_specs=pl.BlockSpec((t, SB_LANES), lambda b, c, i: (b * nq + i, c)),
        out_shape=jax.ShapeDtypeStruct((T, SB_HEADS * SB_HEAD_DIM), BF16),
        compiler_params=_cparams("parallel", "parallel", "arbitrary"),
        name="stick_breaking",
    )(qkv, qkv, qkv)


_CONV_HALO = 8


def _conv_kernel(u_ref, w_ref, b_ref, o_ref, ext_ref):
    ts = u_ref.shape[0]
    s = pl.program_id(1)

    @pl.when(s == 0)
    def _():
        ext_ref[0:_CONV_HALO, :] = jnp.zeros((_CONV_HALO, SSD_CONV_CH), F32)

    @pl.when(s > 0)
    def _():
        ext_ref[0:_CONV_HALO, :] = ext_ref[ts:ts + _CONV_HALO, :]

    ext_ref[_CONV_HALO:_CONV_HALO + ts, :] = u_ref[...].astype(F32)
    cw = 512
    for c0 in range(0, SSD_CONV_CH, cw):
        ext = ext_ref[:, c0:c0 + cw]
        acc = b_ref[:, c0:c0 + cw] + w_ref[SSD_CONV - 1, 0:1, c0:c0 + cw] * ext[_CONV_HALO:]
        for d in range(1, SSD_CONV):
            rolled = pltpu.roll(ext, d, axis=0)
            acc = acc + w_ref[SSD_CONV - 1 - d, 0:1, c0:c0 + cw] * rolled[_CONV_HALO:]
        o_ref[:, c0:c0 + cw] = acc * jax.nn.sigmoid(acc)


def _conv_silu(xbc, conv_w, conv_b, bsz, seq, ts=256):
    T = bsz * seq
    ns = seq // ts
    conv_w = jnp.broadcast_to(conv_w[:, None, :], (SSD_CONV, 8, SSD_CONV_CH))
    return pl.pallas_call(
        _conv_kernel,
        grid=(bsz, ns),
        in_specs=[
            pl.BlockSpec((ts, SSD_CONV_CH), lambda b, s: (b * ns + s, 0)),
            pl.BlockSpec((SSD_CONV, 8, SSD_CONV_CH), lambda b, s: (0, 0, 0)),
            pl.BlockSpec((1, SSD_CONV_CH), lambda b, s: (0, 0)),
        ],
        out_specs=pl.BlockSpec((ts, SSD_CONV_CH), lambda b, s: (b * ns + s, 0)),
        out_shape=jax.ShapeDtypeStruct((T, SSD_CONV_CH), F32),
        scratch_shapes=[pltpu.VMEM((ts + _CONV_HALO, SSD_CONV_CH), F32)],
        compiler_params=_cparams("parallel", "arbitrary"),
        name="conv_silu",
    )(xbc, conv_w, conv_b)


def _ssd_kernel(xs_ref, bm_ref, cm_ref, z_ref, dtr_ref, dtb_ref, a128_ref, dexp_ref, ng_ref, e_ref,
                o_ref, state_ref, y_ref):
    L = SSD_CHUNK
    c = pl.program_id(1)

    @pl.when(c == 0)
    def _():
        state_ref[...] = jnp.zeros_like(state_ref)

    row = lax.broadcasted_iota(jnp.int32, (L, L), 0)
    col = lax.broadcasted_iota(jnp.int32, (L, L), 1)
    incl = (col <= row).astype(F32)
    eye = (col == row).astype(F32)
    lower = col <= row
    lane = lax.broadcasted_iota(jnp.int32, (L, LANES), 1)

    def expand(v):
        v = jnp.where(lane < SSD_HEADS, v, 0.0)
        hi = v.astype(BF16).astype(F32)
        r1 = v - hi
        mid = r1.astype(BF16).astype(F32)
        lo = (r1 - mid).astype(BF16).astype(F32)
        pieces = hi + pltpu.roll(mid, SSD_HEADS, axis=1) + pltpu.roll(lo, 2 * SSD_HEADS, axis=1)
        return _dot(pieces.astype(BF16), e_ref[...])

    dt = jax.nn.softplus(dtr_ref[...] + dtb_ref[...])
    dt_exp = expand(dt)
    cs_col = _dot(incl, dt * a128_ref[...], HIGHEST)
    cs_exp = expand(cs_col)
    cs_row = _dot_nt(eye, cs_col, HIGHEST)

    xs = xs_ref[...]
    xdt = xs * dt_exp
    xdt_b = xdt.astype(BF16)
    cs_last = cs_exp[L - 1:L, :]
    xw_b = (xdt * jnp.exp(cs_last - cs_exp)).astype(BF16)
    grow = jnp.exp(cs_exp)
    grow_last = jnp.exp(cs_last)

    gw = SSD_GROUP_WIDTH
    heads_per_group = SSD_HEADS // SSD_GROUPS
    for g in range(SSD_GROUPS):
        bg = bm_ref[:, g * SSD_STATE:(g + 1) * SSD_STATE]
        cg = cm_ref[:, g * SSD_STATE:(g + 1) * SSD_STATE].astype(BF16)
        cb = _dot_nt(cg, bg.astype(BF16))
        for pair in range(heads_per_group // 2):
            c0 = g * gw + pair * LANES
            x_pair = xdt_b[:, c0:c0 + LANES]
            x_zero = jnp.zeros_like(x_pair)
            x_stack = jnp.concatenate([jnp.where(lane < SSD_HEAD_DIM, x_pair, x_zero),
                                       jnp.where(lane < SSD_HEAD_DIM, x_zero, x_pair)], axis=0)
            m_pair = []
            for sub in range(2):
                hd = g * heads_per_group + pair * 2 + sub
                diff = cs_col[:, hd:hd + 1] - cs_row[hd:hd + 1, :]
                seg = jnp.where(lower, jnp.exp(jnp.minimum(diff, 0.0)), 0.0)
                m_pair.append((cb * seg).astype(BF16))
            y_ref[:, c0:c0 + LANES] = _dot(jnp.concatenate(m_pair, axis=1), x_stack)
        st = state_ref[g]
        y_ref[:, g * gw:(g + 1) * gw] += _dot(cg, st.astype(BF16)) * grow[:, g * gw:(g + 1) * gw]
        state_ref[g] = st * grow_last[:, g * gw:(g + 1) * gw] + _dot(bg.T.astype(BF16), xw_b[:, g * gw:(g + 1) * gw])

    z = z_ref[...].astype(F32)
    y = (y_ref[...] + dexp_ref[...] * xs) * (z * jax.nn.sigmoid(z))
    for g in range(SSD_GROUPS):
        yg = y[:, g * gw:(g + 1) * gw]
        ms = jnp.mean(yg * yg, axis=-1, keepdims=True)
        o_ref[:, g * gw:(g + 1) * gw] = (yg * lax.rsqrt(ms + LN_EPS) * ng_ref[:, g * gw:(g + 1) * gw]).astype(BF16)


def _ssd(xbc_act, z, dt_raw, dt_bias128, a128, d_exp, norm_g, bsz, seq):
    L = SSD_CHUNK
    T = bsz * seq
    nc = seq // L
    nb = SSD_INNER // SSD_GROUP_WIDTH
    rows = jnp.arange(LANES)
    expand = ((rows[:, None] < 3 * SSD_HEADS)
              & ((rows[:, None] % SSD_HEADS) == (jnp.arange(SSD_INNER)[None, :] // SSD_HEAD_DIM))).astype(BF16)
    full = lambda shape: pl.BlockSpec(shape, lambda b, c: (0,) * len(shape))
    return pl.pallas_call(
        _ssd_kernel,
        grid=(bsz, nc),
        in_specs=[
            pl.BlockSpec((L, SSD_INNER), lambda b, c: (b * nc + c, 0)),
            pl.BlockSpec((L, SSD_GROUPS * SSD_STATE), lambda b, c: (b * nc + c, nb)),
            pl.BlockSpec((L, SSD_GROUPS * SSD_STATE), lambda b, c: (b * nc + c, nb + 1)),
            pl.BlockSpec((L, SSD_INNER), lambda b, c: (b * nc + c, 0)),
            pl.BlockSpec((L, LANES), lambda b, c: (b * nc + c, 0)),
            full((1, LANES)), full((1, LANES)), full((1, SSD_INNER)), full((1, SSD_INNER)),
            full((LANES, SSD_INNER)),
        ],
        out_specs=pl.BlockSpec((L, SSD_INNER), lambda b, c: (b * nc + c, 0)),
        out_shape=jax.ShapeDtypeStruct((T, SSD_INNER), BF16),
        scratch_shapes=[
            pltpu.VMEM((SSD_GROUPS, SSD_STATE, SSD_GROUP_WIDTH), F32),
            pltpu.VMEM((L, SSD_INNER), F32),
        ],
        compiler_params=_cparams("parallel", "arbitrary"),
        name="ssd_scan",
    )(xbc_act, xbc_act, xbc_act, z, dt_raw, dt_bias128, a128, d_exp, norm_g, expand)


def _merge_kernel(osb_ref, ossd_ref, gl_ref, h_ref, bg_ref, wsb_ref, wssd_ref, wmix_ref, g_ref, b_ref, o_ref):
    gates = jax.nn.sigmoid(gl_ref[...].astype(F32) + bg_ref[...])
    merged = (gates[:, :D_MODEL] * _dot(osb_ref[...], wsb_ref[...])
              + gates[:, D_MODEL:] * _dot(ossd_ref[...], wssd_ref[...]))
    mix = _dot(merged.astype(BF16), wmix_ref[...])
    o_ref[...] = _layer_norm(DN_ALPHA * h_ref[...] + mix, g_ref[...], b_ref[...])


def _merge(o_sb, o_ssd, gate_logits, h, b_gate, w_sb, w_ssd, w_mix, ln_g, ln_b, tm=256):
    T = h.shape[0]
    full = lambda shape: pl.BlockSpec(shape, lambda i: (0,) * len(shape))
    return pl.pallas_call(
        _merge_kernel,
        grid=(T // tm,),
        in_specs=[
            pl.BlockSpec((tm, D_MODEL), lambda i: (i, 0)),
            pl.BlockSpec((tm, SSD_INNER), lambda i: (i, 0)),
            pl.BlockSpec((tm, 2 * D_MODEL), lambda i: (i, 0)),
            pl.BlockSpec((tm, D_MODEL), lambda i: (i, 0)),
            full((1, 2 * D_MODEL)), full((D_MODEL, D_MODEL)), full((SSD_INNER, D_MODEL)), full((D_MODEL, D_MODEL)),
            full((1, D_MODEL)), full((1, D_MODEL)),
        ],
        out_specs=pl.BlockSpec((tm, D_MODEL), lambda i: (i, 0)),
        out_shape=jax.ShapeDtypeStruct((T, D_MODEL), F32),
        compiler_params=_cparams("parallel"),
        name="merge",
    )(o_sb, o_ssd, gate_logits, h, b_gate, w_sb, w_ssd, w_mix, ln_g, ln_b)


def _kv_kernel(m_ref, wk_ref, wv_ref, k_ref, v_ref):
    mb = m_ref[...].astype(BF16)
    k_ref[...] = _dot(mb, wk_ref[...]).astype(BF16)
    v_ref[...] = _dot(mb, wv_ref[...]).astype(BF16)


def _kv_proj(mem2, w_xk, w_xv, tm=512):
    M = mem2.shape[0]
    full = lambda shape: pl.BlockSpec(shape, lambda i: (0,) * len(shape))
    return pl.pallas_call(
        _kv_kernel,
        grid=(M // tm,),
        in_specs=[pl.BlockSpec((tm, D_MODEL), lambda i: (i, 0)), full((D_MODEL, D_MODEL)), full((D_MODEL, D_MODEL))],
        out_specs=[pl.BlockSpec((tm, D_MODEL), lambda i: (i, 0))] * 2,
        out_shape=[jax.ShapeDtypeStruct((M, D_MODEL), BF16)] * 2,
        compiler_params=_cparams("parallel"),
        name="kv_proj",
    )(mem2, w_xk, w_xv)


def _pack_bf16_pairs(x):
    bits = lax.bitcast_convert_type(x.astype(BF16).astype(F32), jnp.uint32)
    half = x.shape[1] // 2
    return (bits[:, :half] >> 16) | (bits[:, half:] & jnp.uint32(0xFFFF0000))


def _unpack_bf16_pairs(u):
    lo = lax.bitcast_convert_type(u << 16, F32)
    hi = lax.bitcast_convert_type(u & jnp.uint32(0xFFFF0000), F32)
    return jnp.concatenate([lo, hi], axis=1).astype(BF16)


def _xattn_kernel(h_ref, k_ref, v_ref, wq_ref, wo_ref, g_ref, b_ref, o_ref, op_ref):
    h = h_ref[...]
    q = (_dot(h.astype(BF16), wq_ref[...]) * (X_HEAD_DIM ** -0.5)).astype(BF16)
    outs = []
    for hd in range(X_HEADS):
        sl = slice(hd * X_HEAD_DIM, (hd + 1) * X_HEAD_DIM)
        s = _dot_nt(q[:, sl], k_ref[:, sl])
        e = jnp.exp(s - jnp.max(s, axis=-1, keepdims=True))
        p = e / jnp.sum(e, axis=-1, keepdims=True)
        outs.append(_dot(p.astype(BF16), v_ref[:, sl]))
    o = jnp.concatenate(outs, axis=-1).astype(BF16)
    h2 = _layer_norm(DN_ALPHA * h + _dot(o, wo_ref[...]), g_ref[...], b_ref[...])
    o_ref[...] = h2
    op_ref[...] = _pack_bf16_pairs(h2)


def _xattn(h1, k, v, w_xq, w_xo, ln_g, ln_b, bsz, seq, tm=256):
    T = bsz * seq
    nt = seq // tm
    full = lambda shape: pl.BlockSpec(shape, lambda b, i: (0,) * len(shape))
    return pl.pallas_call(
        _xattn_kernel,
        grid=(bsz, nt),
        in_specs=[
            pl.BlockSpec((tm, D_MODEL), lambda b, i: (b * nt + i, 0)),
            pl.BlockSpec((MEM_LEN, D_MODEL), lambda b, i: (b, 0)),
            pl.BlockSpec((MEM_LEN, D_MODEL), lambda b, i: (b, 0)),
            full((D_MODEL, D_MODEL)), full((D_MODEL, D_MODEL)), full((1, D_MODEL)), full((1, D_MODEL)),
        ],
        out_specs=[pl.BlockSpec((tm, D_MODEL), lambda b, i: (b * nt + i, 0)),
                   pl.BlockSpec((tm, D_MODEL // 2), lambda b, i: (b * nt + i, 0))],
        out_shape=[jax.ShapeDtypeStruct((T, D_MODEL), F32), jax.ShapeDtypeStruct((T, D_MODEL // 2), jnp.uint32)],
        compiler_params=_cparams("parallel", "parallel"),
        name="xattn",
    )(h1, k, v, w_xq, w_xo, ln_g, ln_b)


def _router_kernel(h_ref, w_ref, b_ref, idx_ref, p_ref, rank_ref, cnt_ref, run_ref):
    tm = h_ref.shape[0]
    i = pl.program_id(0)

    @pl.when(i == 0)
    def _():
        run_ref[...] = jnp.zeros_like(run_ref)

    lane = lax.broadcasted_iota(jnp.int32, (tm, LANES), 1)
    lane_f = lane.astype(F32)
    neg = jnp.float32(-jnp.inf)
    logits = _dot(h_ref[...], w_ref[...], HIGHEST) + b_ref[...]
    work = jnp.where(lane < N_EXPERTS, logits, neg)
    vals, hots = [], []
    idx_out = jnp.zeros((tm, LANES), F32)
    for k in range(TOP_K):
        m = jnp.max(work, axis=-1, keepdims=True)
        first = jnp.min(jnp.where(work == m, lane_f, float(LANES)), axis=-1, keepdims=True)
        hot = lane_f == first
        vals.append(m)
        hots.append(hot)
        idx_out = jnp.where(lane == k, first, idx_out)
        work = jnp.where(hot, neg, work)
    exps = [jnp.exp(v - vals[0]) for v in vals]
    denom = exps[0] + exps[1] + exps[2] + exps[3]
    member = jnp.zeros((tm, LANES), F32)
    p_out = jnp.zeros((tm, LANES), F32)
    for k in range(TOP_K):
        member = jnp.where(hots[k], 1.0, member)
        p_out = jnp.where(lane == k, exps[k] / denom, p_out)
    row = lax.broadcasted_iota(jnp.int32, (tm, tm), 0)
    col = lax.broadcasted_iota(jnp.int32, (tm, tm), 1)
    before = (col < row).astype(BF16)
    earlier = _dot(before, member.astype(BF16)) + run_ref[...]
    rank_out = jnp.zeros((tm, LANES), F32)
    for k in range(TOP_K):
        r = jnp.sum(jnp.where(hots[k], earlier, 0.0), axis=-1, keepdims=True)
        rank_out = jnp.where(lane == k, r, rank_out)
    run = run_ref[...] + jnp.sum(member, axis=0, keepdims=True)
    run_ref[...] = run
    cnt_ref[...] = run
    idx_ref[...] = idx_out.astype(jnp.int32)
    p_ref[...] = p_out
    rank_ref[...] = rank_out.astype(jnp.int32)


def _router(h2, w_r, b_r, tm=512):
    T = h2.shape[0]
    full = lambda shape: pl.BlockSpec(shape, lambda i: (0,) * len(shape))
    tok = pl.BlockSpec((tm, LANES), lambda i: (i, 0))
    return pl.pallas_call(
        _router_kernel,
        grid=(T // tm,),
        in_specs=[pl.BlockSpec((tm, D_MODEL), lambda i: (i, 0)), full((D_MODEL, LANES)), full((1, LANES))],
        out_specs=[tok, tok, tok, full((1, LANES))],
        out_shape=[
            jax.ShapeDtypeStruct((T, LANES), jnp.int32),
            jax.ShapeDtypeStruct((T, LANES), F32),
            jax.ShapeDtypeStruct((T, LANES), jnp.int32),
            jax.ShapeDtypeStruct((1, LANES), F32),
        ],
        scratch_shapes=[pltpu.VMEM((1, LANES), F32)],
        compiler_params=_cparams("arbitrary"),
        name="router",
    )(h2, w_r, b_r)


def _dispatch_kernel(dest_ref, h_ref, xin_ref, x_ref, sem):
    del xin_ref
    td = h_ref.shape[0]

    for t in range(td):
        for k in range(TOP_K):
            d = dest_ref[t * TOP_K + k]
            pltpu.make_async_copy(h_ref.at[pl.ds(t, 1), :], x_ref.at[pl.ds(d, 1), :], sem).start(priority=k % 2)

    def wait(t, _):
        for k in range(TOP_K):
            pltpu.make_async_copy(h_ref.at[pl.ds(0, 1), :], x_ref.at[pl.ds(0, 1), :], sem).wait()
        return 0

    lax.fori_loop(0, td, wait, 0, unroll=8)


def _dispatch(dest_flat, h2_packed, n_rows, td=256):
    T, width = h2_packed.shape
    x0 = jnp.zeros((n_rows, width), h2_packed.dtype)
    return pl.pallas_call(
        _dispatch_kernel,
        grid=(T // td,),
        in_specs=[
            pl.BlockSpec((td * TOP_K,), lambda i: (i,), memory_space=pltpu.SMEM),
            pl.BlockSpec((td, width), lambda i: (i, 0)),
            pl.BlockSpec(memory_space=pl.ANY),
        ],
        out_specs=pl.BlockSpec(memory_space=pl.ANY),
        out_shape=jax.ShapeDtypeStruct((n_rows, width), h2_packed.dtype),
        scratch_shapes=[pltpu.SemaphoreType.DMA(())],
        input_output_aliases={2: 0},
        compiler_params=_cparams("arbitrary"),
        name="moe_dispatch",
    )(dest_flat, h2_packed, x0)


def _expert_kernel(be_ref, nused_ref, x_ref, wg_ref, bg_ref, wu_ref, bu_ref, wd_ref, bd_ref, y_ref,
                   wg_bf, wu_bf, wd_bf):
    blk = pl.program_id(0)
    new_expert = (blk == 0) | (be_ref[blk] != be_ref[jnp.maximum(blk - 1, 0)])

    @pl.when(new_expert & (blk < nused_ref[0]))
    def _():
        wg_bf[...] = wg_ref[0].astype(BF16)
        wu_bf[...] = wu_ref[0].astype(BF16)
        wd_bf[...] = wd_ref[0].astype(BF16)

    @pl.when(blk < nused_ref[0])
    def _():
        xb = _unpack_bf16_pairs(x_ref[...])
        g = jnp.minimum(_dot(xb, wg_bf[...]) + bg_ref[0], SWIGLU_LIMIT)
        u = jnp.clip(_dot(xb, wu_bf[...]) + bu_ref[0], -SWIGLU_LIMIT, SWIGLU_LIMIT)
        act = (u + 1.0) * g * jax.nn.sigmoid(SWIGLU_ALPHA * g)
        y_ref[...] = _dot(act.astype(BF16), wd_bf[...]) + bd_ref[0]

    @pl.when(blk >= nused_ref[0])
    def _():
        y_ref[...] = jnp.zeros_like(y_ref)


def _experts(block_e, n_used, x_rows, w_gate, b_gate, w_up, b_up, w_down, b_down):
    n_rows = x_rows.shape[0]
    bm = EXPERT_ROWS
    wspec = pl.BlockSpec((1, D_MODEL, D_MODEL), lambda i, be, nu: (be[i], 0, 0))
    bspec = pl.BlockSpec((1, 1, D_MODEL), lambda i, be, nu: (be[i], 0, 0))
    return pl.pallas_call(
        _expert_kernel,
        grid_spec=pltpu.PrefetchScalarGridSpec(
            num_scalar_prefetch=2,
            grid=(n_rows // bm,),
            in_specs=[pl.BlockSpec((bm, D_MODEL // 2), lambda i, be, nu: (i, 0)),
                      wspec, bspec, wspec, bspec, wspec, bspec],
            out_specs=pl.BlockSpec((bm, D_MODEL), lambda i, be, nu: (i, 0)),
            scratch_shapes=[pltpu.VMEM((D_MODEL, D_MODEL), BF16)] * 3,
        ),
        out_shape=jax.ShapeDtypeStruct((n_rows, D_MODEL), F32),
        compiler_params=_cparams("arbitrary"),
        name="moe_experts",
    )(block_e, n_used, x_rows, w_gate, b_gate, w_up, b_up, w_down, b_down)


def _combine_kernel(dcur_ref, dnext_ref, y_ref, p_ref, h_ref, g_ref, b_ref, o_ref, buf, sem):
    tc = h_ref.shape[0]
    i = pl.program_id(0)
    n = pl.num_programs(0)

    def issue(dref, slot):
        for t in range(tc):
            for k in range(TOP_K):
                d = dref[t * TOP_K + k]
                pltpu.make_async_copy(y_ref.at[pl.ds(d, 1), :], buf.at[slot, k, pl.ds(t, 1), :],
                                      sem.at[slot]).start(priority=k % 2)

    slot = i % 2

    @pl.when(i == 0)
    def _():
        issue(dcur_ref, 0)

    @pl.when((i + 1 < n) & (slot == 1))
    def _():
        issue(dnext_ref, 0)

    @pl.when((i + 1 < n) & (slot == 0))
    def _():
        issue(dnext_ref, 1)

    def wait_body(t, _):
        for k in range(TOP_K):
            pltpu.make_async_copy(y_ref.at[pl.ds(0, 1), :], buf.at[slot, k, pl.ds(0, 1), :], sem.at[slot]).wait()
        return 0

    lax.fori_loop(0, tc, wait_body, 0, unroll=8)

    p = p_ref[...]
    y = p[:, 0:1] * buf[slot, 0]
    for k in range(1, TOP_K):
        y = y + p[:, k:k + 1] * buf[slot, k]
    o_ref[...] = _layer_norm(DN_ALPHA * h_ref[...] + y, g_ref[...], b_ref[...])


def _combine(dest_flat, y_rows, probs, h2, ln_g, ln_b, tc=128):
    T = h2.shape[0]
    nt = T // tc
    full = lambda shape: pl.BlockSpec(shape, lambda i: (0,) * len(shape))
    return pl.pallas_call(
        _combine_kernel,
        grid=(nt,),
        in_specs=[
            pl.BlockSpec((tc * TOP_K,), lambda i: (i,), memory_space=pltpu.SMEM),
            pl.BlockSpec((tc * TOP_K,), lambda i: (jnp.minimum(i + 1, nt - 1),), memory_space=pltpu.SMEM),
            pl.BlockSpec(memory_space=pl.ANY),
            pl.BlockSpec((tc, LANES), lambda i: (i, 0)),
            pl.BlockSpec((tc, D_MODEL), lambda i: (i, 0)),
            full((1, D_MODEL)), full((1, D_MODEL)),
        ],
        out_specs=pl.BlockSpec((tc, D_MODEL), lambda i: (i, 0)),
        out_shape=jax.ShapeDtypeStruct((T, D_MODEL), F32),
        scratch_shapes=[pltpu.VMEM((2, TOP_K, tc, D_MODEL), F32), pltpu.SemaphoreType.DMA((2,))],
        compiler_params=_cparams("arbitrary"),
        name="moe_combine",
    )(dest_flat, dest_flat, y_rows, probs, h2, ln_g, ln_b)


def _pad_lanes(v, fill=0.0):
    v = v.reshape(1, -1)
    return jnp.pad(v, ((0, 0), (0, LANES - v.shape[1])), constant_values=fill)


def _mixer_stage(x2, bsz, seq, ln_in_g, ln_in_b, w_in, b_branch_gate, conv_w, conv_b, dt_bias, a_log, d_skip,
                 ssd_norm_g, w_sb, w_ssd, w_mix_out, ln1_g, ln1_b):
    n_lin = 3 * D_MODEL + SSD_INNER + SSD_CONV_CH
    w_main = jnp.concatenate([w_in[:, :n_lin], w_in[:, n_lin + SSD_HEADS:]], axis=1).astype(BF16)
    w_dt = jnp.pad(w_in[:, n_lin:n_lin + SSD_HEADS], ((0, 0), (0, LANES - SSD_HEADS))).astype(BF16)
    h, qkv, z, xbc, gate_logits, dt_raw = _inproj(x2, ln_in_g.reshape(1, -1), ln_in_b.reshape(1, -1), w_main, w_dt)

    o_sb = _stick_breaking(qkv, bsz, seq)

    xbc_act = _conv_silu(xbc, conv_w, conv_b.reshape(1, -1), bsz, seq)
    a = -jnp.exp(a_log.astype(F32))
    o_ssd = _ssd(xbc_act, z, dt_raw, _pad_lanes(dt_bias), _pad_lanes(a),
                 jnp.repeat(d_skip, SSD_HEAD_DIM).reshape(1, -1), ssd_norm_g.reshape(1, -1), bsz, seq)

    return _merge(o_sb, o_ssd, gate_logits, h, b_branch_gate.reshape(1, -1), w_sb.astype(BF16), w_ssd.astype(BF16),
                  w_mix_out.astype(BF16), ln1_g.reshape(1, -1), ln1_b.reshape(1, -1))


def _xattn_stage(h1, mem2, bsz, seq, w_xq, w_xk, w_xv, w_xo, ln2_g, ln2_b):
    k, v = _kv_proj(mem2, w_xk.astype(BF16), w_xv.astype(BF16))
    return _xattn(h1, k, v, w_xq.astype(BF16), w_xo.astype(BF16), ln2_g.reshape(1, -1), ln2_b.reshape(1, -1),
                  bsz, seq)


def _moe_stage(h2, h2_packed, w_router, b_router, w_e_gate, b_e_gate, w_e_up, b_e_up, w_e_down, b_e_down, ln3_g,
               ln3_b):
    T = h2.shape[0]
    bm = EXPERT_ROWS
    w_r = jnp.pad(w_router, ((0, 0), (0, LANES - N_EXPERTS)))
    idx_p, probs, rank_p, counts_p = _router(h2, w_r, _pad_lanes(b_router))
    idx = idx_p[:, :TOP_K]
    rank = rank_p[:, :TOP_K]
    counts = counts_p[0, :N_EXPERTS].astype(jnp.int32)
    padded = (counts + bm - 1) // bm * bm
    end_padded = jnp.cumsum(padded)
    start_padded = end_padded - padded
    onehot = idx[:, :, None] == jnp.arange(N_EXPERTS, dtype=jnp.int32)[None, None, :]
    dest = jnp.sum(jnp.where(onehot, start_padded[None, None, :], 0), axis=-1) + rank
    dest_flat = dest.reshape(-1).astype(jnp.int32)
    n_blocks = -(-(T * TOP_K + N_EXPERTS * (bm - 1)) // bm)
    n_rows = n_blocks * bm
    blk_start = jnp.arange(n_blocks, dtype=jnp.int32) * bm
    block_e = jnp.minimum(jnp.sum(blk_start[:, None] >= end_padded[None, :], axis=-1), N_EXPERTS - 1).astype(jnp.int32)
    n_used = (end_padded[-1:] // bm).astype(jnp.int32)

    x_rows = _dispatch(dest_flat, h2_packed, n_rows)
    y_rows = _experts(block_e, n_used, x_rows, w_e_gate, b_e_gate[:, None, :], w_e_up, b_e_up[:, None, :],
                      w_e_down, b_e_down[:, None, :])
    return _combine(dest_flat, y_rows, probs, h2, ln3_g.reshape(1, -1), ln3_b.reshape(1, -1))


def kernel(x, mem, ln_in_g, ln_in_b, w_in, b_branch_gate, conv_w, conv_b, dt_bias, a_log, d_skip, ssd_norm_g, w_sb,
           w_ssd, w_mix_out, ln1_g, ln1_b, w_xq, w_xk, w_xv, w_xo, ln2_g, ln2_b, w_router, b_router, w_e_gate,
           b_e_gate, w_e_up, b_e_up, w_e_down, b_e_down, ln3_g, ln3_b):
    bsz, seq, _ = x.shape
    depth = w_in.shape[0]
    x2 = x.reshape(bsz * seq, D_MODEL)
    mem2 = mem.reshape(bsz * mem.shape[1], D_MODEL)
    assert depth == 1, "the entry LayerNorm is fused into the single layer's input projection"
    l = 0
    h1 = _mixer_stage(x2, bsz, seq, ln_in_g, ln_in_b, w_in[l], b_branch_gate[l], conv_w[l], conv_b[l], dt_bias[l],
                      a_log[l], d_skip[l], ssd_norm_g[l], w_sb[l], w_ssd[l], w_mix_out[l], ln1_g[l], ln1_b[l])
    h2, h2_packed = _xattn_stage(h1, mem2, bsz, seq, w_xq[l], w_xk[l], w_xv[l], w_xo[l], ln2_g[l], ln2_b[l])
    h = _moe_stage(h2, h2_packed, w_router[l], b_router[l], w_e_gate[l], b_e_gate[l], w_e_up[l], b_e_up[l], w_e_down[l],
                   b_e_down[l], ln3_g[l], ln3_b[l])
    return h.reshape(bsz, seq, D_MODEL)
```

```python
import functools

import jax
import jax.numpy as jnp
from jax import lax
from jax.experimental import pallas as pl
from jax.experimental.pallas import tpu as pltpu

F32 = jnp.float32
BF16 = jnp.bfloat16
HIGHEST = lax.Precision.HIGHEST

D_MODEL = 1024
LN_EPS = 1e-5
DN_ALPHA = 2.0 ** 0.25
LANES = 128

SB_HEADS = 16
SB_HEAD_DIM = 64
SB_TILE = 256
SB_LANES = 256
SB_GROUP = SB_LANES // SB_HEAD_DIM
LOG2E = 1.4426950408889634
SB_SKIP_BITS = 152.0

SSD_INNER = 2048
SSD_HEADS = 32
SSD_HEAD_DIM = 64
SSD_GROUPS = 4
SSD_GROUP_WIDTH = SSD_INNER // SSD_GROUPS
SSD_STATE = 128
SSD_CONV = 4
SSD_CONV_CH = SSD_INNER + 2 * SSD_GROUPS * SSD_STATE
SSD_CHUNK = 128

X_HEADS = 4
X_HEAD_DIM = 256
MEM_LEN = 256

N_EXPERTS = 32
TOP_K = 4
SWIGLU_LIMIT = 7.0
SWIGLU_ALPHA = 1.702
EXPERT_ROWS = 512

VMEM_LIMIT = 52 * 1024 * 1024


def _cparams(*sem):
    return pltpu.CompilerParams(dimension_semantics=sem, vmem_limit_bytes=VMEM_LIMIT)


def _layer_norm(x, g, b):
    mu = jnp.mean(x, axis=-1, keepdims=True)
    xc = x - mu
    var = jnp.mean(xc * xc, axis=-1, keepdims=True)
    return xc * lax.rsqrt(var + LN_EPS) * g + b


def _dot(a, b, precision=None):
    return jnp.dot(a, b, preferred_element_type=F32, precision=precision)


def _dot_nt(a, b, precision=None):
    return lax.dot_general(a, b, (((1,), (1,)), ((), ())), preferred_element_type=F32, precision=precision)


_IN_TN = 1024
_N_QKV, _N_Z, _N_XBC, _N_GATE = (3 * D_MODEL // _IN_TN, SSD_INNER // _IN_TN, SSD_CONV_CH // _IN_TN,
                                 2 * D_MODEL // _IN_TN)
_QKV_SPLIT = D_MODEL // _IN_TN


def _inproj_kernel(x_ref, g_ref, b_ref, w_ref, wdt_ref, h_ref, qkv_ref, z_ref, xbc_ref, gate_ref, dt_ref, hb_ref):
    j = pl.program_id(1)

    @pl.when(j == 0)
    def _():
        h = _layer_norm(x_ref[...], g_ref[...], b_ref[...])
        h_ref[...] = h
        hb = h.astype(BF16)
        hb_ref[...] = hb
        dt_ref[...] = _dot(hb, wdt_ref[...])

    @pl.when(j < _N_QKV)
    def _():
        for c in range(_IN_TN // SB_LANES):
            sl = slice(c * SB_LANES, (c + 1) * SB_LANES)
            qkv_ref[0, c] = _dot(hb_ref[...], w_ref[:, sl]).astype(BF16)

    @pl.when((j >= _N_QKV) & (j < _N_QKV + _N_Z))
    def _():
        z_ref[...] = _dot(hb_ref[...], w_ref[...]).astype(BF16)

    @pl.when((j >= _N_QKV + _N_Z) & (j < _N_QKV + _N_Z + _N_XBC))
    def _():
        xbc_ref[...] = _dot(hb_ref[...], w_ref[...]).astype(BF16)

    @pl.when(j >= _N_QKV + _N_Z + _N_XBC)
    def _():
        gate_ref[...] = _dot(hb_ref[...], w_ref[...]).astype(BF16)


def _inproj(x2, ln_g, ln_b, w_main, w_dt, tm=1024):
    T = x2.shape[0]
    tm = min(tm, T)
    tn = _IN_TN
    nj = _N_QKV + _N_Z + _N_XBC + _N_GATE
    o_z, o_xbc, o_gate = _N_QKV, _N_QKV + _N_Z, _N_QKV + _N_Z + _N_XBC

    def qkv_map(i, j):
        jq = jnp.minimum(j, _N_QKV - 1)
        return (jq // _QKV_SPLIT, jq % _QKV_SPLIT, i, 0)

    return pl.pallas_call(
        _inproj_kernel,
        grid=(T // tm, nj),
        in_specs=[
            pl.BlockSpec((tm, D_MODEL), lambda i, j: (i, 0)),
            pl.BlockSpec((1, D_MODEL), lambda i, j: (0, 0)),
            pl.BlockSpec((1, D_MODEL), lambda i, j: (0, 0)),
            pl.BlockSpec((D_MODEL, tn), lambda i, j: (0, j)),
            pl.BlockSpec((D_MODEL, LANES), lambda i, j: (0, 0)),
        ],
        out_specs=[
            pl.BlockSpec((tm, D_MODEL), lambda i, j: (i, 0)),
            pl.BlockSpec((1, tn // SB_LANES, tm, SB_LANES), qkv_map),
            pl.BlockSpec((tm, tn), lambda i, j: (i, jnp.clip(j - o_z, 0, _N_Z - 1))),
            pl.BlockSpec((tm, tn), lambda i, j: (i, jnp.clip(j - o_xbc, 0, _N_XBC - 1))),
            pl.BlockSpec((tm, tn), lambda i, j: (i, jnp.clip(j - o_gate, 0, _N_GATE - 1))),
            pl.BlockSpec((tm, LANES), lambda i, j: (i, 0)),
        ],
        out_shape=[
            jax.ShapeDtypeStruct((T, D_MODEL), F32),
            jax.ShapeDtypeStruct((3, D_MODEL // SB_LANES, T, SB_LANES), BF16),
            jax.ShapeDtypeStruct((T, SSD_INNER), BF16),
            jax.ShapeDtypeStruct((T, SSD_CONV_CH), BF16),
            jax.ShapeDtypeStruct((T, 2 * D_MODEL), BF16),
            jax.ShapeDtypeStruct((T, LANES), F32),
        ],
        scratch_shapes=[pltpu.VMEM((tm, D_MODEL), BF16)],
        compiler_params=_cparams("parallel", "arbitrary"),
        name="inproj",
    )(x2, ln_g, ln_b, w_main, w_dt)


def _sb_kernel(q_ref, k_ref, v_ref, o_ref):
    t = SB_TILE
    G = SB_GROUP
    i = pl.program_id(2)
    lane = lax.broadcasted_iota(jnp.int32, (t, SB_LANES), 1)
    head_lanes = [(lane >= h * SB_HEAD_DIM) & (lane < (h + 1) * SB_HEAD_DIM) for h in range(G)]
    qs = q_ref[0, 0] * BF16(SB_HEAD_DIM ** -0.5)
    zero = jnp.zeros_like(qs)
    q_stack = jnp.concatenate([jnp.where(m, qs, zero) for m in head_lanes], axis=0)
    row = lax.broadcasted_iota(jnp.int32, (G * t, t), 0) & (t - 1)
    col = lax.broadcasted_iota(jnp.int32, (G * t, t), 1)
    causal = col < row
    krow = lax.broadcasted_iota(jnp.int32, (2 * t, t), 0) & (t - 1)
    kcol = lax.broadcasted_iota(jnp.int32, (2 * t, t), 1)
    later2 = (krow > kcol).astype(BF16)

    def block(jb, c, acc, diag):
        k = k_ref[0, 0, pl.ds(pl.multiple_of(jb * t, t), t), :]
        v = v_ref[0, 0, pl.ds(pl.multiple_of(jb * t, t), t), :]
        vzero = jnp.zeros_like(v)
        v_stack = jnp.concatenate([jnp.where(m, v, vzero) for m in head_lanes], axis=0)
        z2 = _dot_nt(q_stack, k) * LOG2E
        neg_abs = lax.bitcast_convert_type(lax.bitcast_convert_type(z2, jnp.uint32) | jnp.uint32(0x80000000), F32)
        sp2 = jnp.maximum(z2, 0.0) + jnp.log2(1.0 + jnp.exp2(neg_abs))
        spm = jnp.where(causal, sp2, 0.0) if diag else sp2
        hi = spm.astype(BF16)
        lo = (spm - hi.astype(F32)).astype(BF16)
        after = _dot(jnp.concatenate([hi, lo], axis=1), later2)
        w = jnp.exp2(z2 - sp2 - after - c)
        if diag:
            w = jnp.where(causal, w, 0.0)
        wb = w.astype(BF16)
        w_cat = jnp.concatenate([wb[h * t:(h + 1) * t] for h in range(G)], axis=1)
        return c + jnp.sum(spm, axis=1, keepdims=True), acc + _dot(w_cat, v_stack)

    c, acc = block(i, jnp.zeros((G * t, 1), F32), jnp.zeros((t, SB_LANES), F32), True)

    def cond(carry):
        jb, cmin, _, _ = carry
        return (jb >= 0) & (cmin < SB_SKIP_BITS)

    def body(carry):
        jb, _, c, acc = carry
        c, acc = block(jb, c, acc, False)
        return jb - 1, jnp.min(c), c, acc

    _, _, _, acc = lax.while_loop(cond, body, (i - 1, jnp.min(c), c, acc))
    o_ref[...] = acc.astype(BF16)


def _stick_breaking(qkv, bsz, seq):
    t = SB_TILE
    nq = seq // t
    T = bsz * seq
    return pl.pallas_call(
        _sb_kernel,
        grid=(bsz, SB_HEADS // SB_GROUP, nq),
        in_specs=[
            pl.BlockSpec((1, 1, t, SB_LANES), lambda b, c, i: (0, c, b * nq + i, 0)),
            pl.BlockSpec((1, 1, seq, SB_LANES), lambda b, c, i: (1, c, b, 0)),
            pl.BlockSpec((1, 1, seq, SB_LANES), lambda b, c, i: (2, c, b, 0)),
        ],
        out_specs=pl.BlockSpec((t, SB_LANES), lambda b, c, i: (b * nq + i, c)),
        out_shape=jax.ShapeDtypeStruct((T, SB_HEADS * SB_HEAD_DIM), BF16),
        compiler_params=_cparams("parallel", "parallel", "arbitrary"),
        name="stick_breaking",
    )(qkv, qkv, qkv)


_CONV_HALO = 8


def _conv_kernel(u_ref, w_ref, b_ref, o_ref, ext_ref):
    ts = u_ref.shape[0]
    s = pl.program_id(1)

    @pl.when(s == 0)
    def _():
        ext_ref[0:_CONV_HALO, :] = jnp.zeros((_CONV_HALO, SSD_CONV_CH), F32)

    @pl.when(s > 0)
    def _():
        ext_ref[0:_CONV_HALO, :] = ext_ref[ts:ts + _CONV_HALO, :]

    ext_ref[_CONV_HALO:_CONV_HALO + ts, :] = u_ref[...].astype(F32)
    cw = 512
    for c0 in range(0, SSD_CONV_CH, cw):
        ext = ext_ref[:, c0:c0 + cw]
        acc = b_ref[:, c0:c0 + cw] + w_ref[SSD_CONV - 1, 0:1, c0:c0 + cw] * ext[_CONV_HALO:]
        for d in range(1, SSD_CONV):
            rolled = pltpu.roll(ext, d, axis=0)
            acc = acc + w_ref[SSD_CONV - 1 - d, 0:1, c0:c0 + cw] * rolled[_CONV_HALO:]
        o_ref[:, c0:c0 + cw] = acc * jax.nn.sigmoid(acc)


def _conv_silu(xbc, conv_w, conv_b, bsz, seq, ts=512):
    T = bsz * seq
    ns = seq // ts
    conv_w = jnp.broadcast_to(conv_w[:, None, :], (SSD_CONV, 8, SSD_CONV_CH))
    return pl.pallas_call(
        _conv_kernel,
        grid=(bsz, ns),
        in_specs=[
            pl.BlockSpec((ts, SSD_CONV_CH), lambda b, s: (b * ns + s, 0)),
            pl.BlockSpec((SSD_CONV, 8, SSD_CONV_CH), lambda b, s: (0, 0, 0)),
            pl.BlockSpec((1, SSD_CONV_CH), lambda b, s: (0, 0)),
        ],
        out_specs=pl.BlockSpec((ts, SSD_CONV_CH), lambda b, s: (b * ns + s, 0)),
        out_shape=jax.ShapeDtypeStruct((T, SSD_CONV_CH), F32),
        scratch_shapes=[pltpu.VMEM((ts + _CONV_HALO, SSD_CONV_CH), F32)],
        compiler_params=_cparams("parallel", "arbitrary"),
        name="conv_silu",
    )(xbc, conv_w, conv_b)


def _ssd_kernel(xs_ref, bm_ref, cm_ref, z_ref, dtr_ref, dtb_ref, a128_ref, dexp_ref, ng_ref, e_ref,
                o_ref, state_ref, y_ref):
    L = SSD_CHUNK
    c = pl.program_id(1)

    @pl.when(c == 0)
    def _():
        state_ref[...] = jnp.zeros_like(state_ref)

    row = lax.broadcasted_iota(jnp.int32, (L, L), 0)
    col = lax.broadcasted_iota(jnp.int32, (L, L), 1)
    incl = (col <= row).astype(F32)
    eye = (col == row).astype(F32)
    lower = col <= row
    lane = lax.broadcasted_iota(jnp.int32, (L, LANES), 1)

    def expand(v):
        v = jnp.where(lane < SSD_HEADS, v, 0.0)
        hi = v.astype(BF16).astype(F32)
        r1 = v - hi
        mid = r1.astype(BF16).astype(F32)
        lo = (r1 - mid).astype(BF16).astype(F32)
        pieces = hi + pltpu.roll(mid, SSD_HEADS, axis=1) + pltpu.roll(lo, 2 * SSD_HEADS, axis=1)
        return _dot(pieces.astype(BF16), e_ref[...])

    dt = jax.nn.softplus(dtr_ref[...] + dtb_ref[...])
    dt_exp = expand(dt)
    cs_col = _dot(incl, dt * a128_ref[...], HIGHEST)
    cs_exp = expand(cs_col)
    cs_row = _dot_nt(eye, cs_col, HIGHEST)

    xs = xs_ref[...]
    xdt = xs * dt_exp
    xdt_b = xdt.astype(BF16)
    cs_last = cs_exp[L - 1:L, :]
    xw_b = (xdt * jnp.exp(cs_last - cs_exp)).astype(BF16)
    grow = jnp.exp(cs_exp)
    grow_last = jnp.exp(cs_last)

    gw = SSD_GROUP_WIDTH
    heads_per_group = SSD_HEADS // SSD_GROUPS
    for g in range(SSD_GROUPS):
        bg = bm_ref[:, g * SSD_STATE:(g + 1) * SSD_STATE]
        cg = cm_ref[:, g * SSD_STATE:(g + 1) * SSD_STATE].astype(BF16)
        cb = _dot_nt(cg, bg.astype(BF16))
        for pair in range(heads_per_group // 2):
            c0 = g * gw + pair * LANES
            x_pair = xdt_b[:, c0:c0 + LANES]
            x_zero = jnp.zeros_like(x_pair)
            x_stack = jnp.concatenate([jnp.where(lane < SSD_HEAD_DIM, x_pair, x_zero),
                                       jnp.where(lane < SSD_HEAD_DIM, x_zero, x_pair)], axis=0)
            m_pair = []
            for sub in range(2):
                hd = g * heads_per_group + pair * 2 + sub
                diff = cs_col[:, hd:hd + 1] - cs_row[hd:hd + 1, :]
                seg = jnp.where(lower, jnp.exp(jnp.minimum(diff, 0.0)), 0.0)
                m_pair.append((cb * seg).astype(BF16))
            y_ref[:, c0:c0 + LANES] = _dot(jnp.concatenate(m_pair, axis=1), x_stack)
        st = state_ref[g]
        y_ref[:, g * gw:(g + 1) * gw] += _dot(cg, st.astype(BF16)) * grow[:, g * gw:(g + 1) * gw]
        state_ref[g] = st * grow_last[:, g * gw:(g + 1) * gw] + _dot(bg.T.astype(BF16), xw_b[:, g * gw:(g + 1) * gw])

    z = z_ref[...].astype(F32)
    y = (y_ref[...] + dexp_ref[...] * xs) * (z * jax.nn.sigmoid(z))
    for g in range(SSD_GROUPS):
        yg = y[:, g * gw:(g + 1) * gw]
        ms = jnp.mean(yg * yg, axis=-1, keepdims=True)
        o_ref[:, g * gw:(g + 1) * gw] = (yg * lax.rsqrt(ms + LN_EPS) * ng_ref[:, g * gw:(g + 1) * gw]).astype(BF16)


def _ssd(xbc_act, z, dt_raw, dt_bias128, a128, d_exp, norm_g, bsz, seq):
    L = SSD_CHUNK
    T = bsz * seq
    nc = seq // L
    nb = SSD_INNER // SSD_GROUP_WIDTH
    rows = jnp.arange(LANES)
    expand = ((rows[:, None] < 3 * SSD_HEADS)
              & ((rows[:, None] % SSD_HEADS) == (jnp.arange(SSD_INNER)[None, :] // SSD_HEAD_DIM))).astype(BF16)
    full = lambda shape: pl.BlockSpec(shape, lambda b, c: (0,) * len(shape))
    return pl.pallas_call(
        _ssd_kernel,
        grid=(bsz, nc),
        in_specs=[
            pl.BlockSpec((L, SSD_INNER), lambda b, c: (b * nc + c, 0)),
            pl.BlockSpec((L, SSD_GROUPS * SSD_STATE), lambda b, c: (b * nc + c, nb)),
            pl.BlockSpec((L, SSD_GROUPS * SSD_STATE), lambda b, c: (b * nc + c, nb + 1)),
            pl.BlockSpec((L, SSD_INNER), lambda b, c: (b * nc + c, 0)),
            pl.BlockSpec((L, LANES), lambda b, c: (b * nc + c, 0)),
            full((1, LANES)), full((1, LANES)), full((1, SSD_INNER)), full((1, SSD_INNER)),
            full((LANES, SSD_INNER)),
        ],
        out_specs=pl.BlockSpec((L, SSD_INNER), lambda b, c: (b * nc + c, 0)),
        out_shape=jax.ShapeDtypeStruct((T, SSD_INNER), BF16),
        scratch_shapes=[
            pltpu.VMEM((SSD_GROUPS, SSD_STATE, SSD_GROUP_WIDTH), F32),
            pltpu.VMEM((L, SSD_INNER), F32),
        ],
        compiler_params=_cparams("parallel", "arbitrary"),
        name="ssd_scan",
    )(xbc_act, xbc_act, xbc_act, z, dt_raw, dt_bias128, a128, d_exp, norm_g, expand)


def _merge_kernel(osb_ref, ossd_ref, gl_ref, h_ref, bg_ref, wsb_ref, wssd_ref, wmix_ref, g_ref, b_ref, o_ref):
    gates = jax.nn.sigmoid(gl_ref[...].astype(F32) + bg_ref[...])
    merged = (gates[:, :D_MODEL] * _dot(osb_ref[...], wsb_ref[...])
              + gates[:, D_MODEL:] * _dot(ossd_ref[...], wssd_ref[...]))
    mix = _dot(merged.astype(BF16), wmix_ref[...])
    o_ref[...] = _layer_norm(DN_ALPHA * h_ref[...] + mix, g_ref[...], b_ref[...])


def _merge(o_sb, o_ssd, gate_logits, h, b_gate, w_sb, w_ssd, w_mix, ln_g, ln_b, tm=512):
    T = h.shape[0]
    full = lambda shape: pl.BlockSpec(shape, lambda i: (0,) * len(shape))
    return pl.pallas_call(
        _merge_kernel,
        grid=(T // tm,),
        in_specs=[
            pl.BlockSpec((tm, D_MODEL), lambda i: (i, 0)),
            pl.BlockSpec((tm, SSD_INNER), lambda i: (i, 0)),
            pl.BlockSpec((tm, 2 * D_MODEL), lambda i: (i, 0)),
            pl.BlockSpec((tm, D_MODEL), lambda i: (i, 0)),
            full((1, 2 * D_MODEL)), full((D_MODEL, D_MODEL)), full((SSD_INNER, D_MODEL)), full((D_MODEL, D_MODEL)),
            full((1, D_MODEL)), full((1, D_MODEL)),
        ],
        out_specs=pl.BlockSpec((tm, D_MODEL), lambda i: (i, 0)),
        out_shape=jax.ShapeDtypeStruct((T, D_MODEL), F32),
        compiler_params=_cparams("parallel"),
        name="merge",
    )(o_sb, o_ssd, gate_logits, h, b_gate, w_sb, w_ssd, w_mix, ln_g, ln_b)


def _kv_kernel(m_ref, wk_ref, wv_ref, k_ref, v_ref):
    mb = m_ref[...].astype(BF16)
    k_ref[...] = _dot(mb, wk_ref[...]).astype(BF16)
    v_ref[...] = _dot(mb, wv_ref[...]).astype(BF16)


def _kv_proj(mem2, w_xk, w_xv, tm=512):
    M = mem2.shape[0]
    full = lambda shape: pl.BlockSpec(shape, lambda i: (0,) * len(shape))
    return pl.pallas_call(
        _kv_kernel,
        grid=(M // tm,),
        in_specs=[pl.BlockSpec((tm, D_MODEL), lambda i: (i, 0)), full((D_MODEL, D_MODEL)), full((D_MODEL, D_MODEL))],
        out_specs=[pl.BlockSpec((tm, D_MODEL), lambda i: (i, 0))] * 2,
        out_shape=[jax.ShapeDtypeStruct((M, D_MODEL), BF16)] * 2,
        compiler_params=_cparams("parallel"),
        name="kv_proj",
    )(mem2, w_xk, w_xv)


def _pack_bf16_pairs(x):
    bits = lax.bitcast_convert_type(x.astype(BF16).astype(F32), jnp.uint32)
    half = x.shape[1] // 2
    return (bits[:, :half] >> 16) | (bits[:, half:] & jnp.uint32(0xFFFF0000))


def _unpack_bf16_pairs(u):
    lo = lax.bitcast_convert_type(u << 16, F32)
    hi = lax.bitcast_convert_type(u & jnp.uint32(0xFFFF0000), F32)
    return jnp.concatenate([lo, hi], axis=1).astype(BF16)


def _xattn_kernel(h_ref, k_ref, v_ref, wq_ref, wo_ref, g_ref, b_ref, o_ref, op_ref):
    h = h_ref[...]
    q = (_dot(h.astype(BF16), wq_ref[...]) * (X_HEAD_DIM ** -0.5)).astype(BF16)
    outs = []
    for hd in range(X_HEADS):
        sl = slice(hd * X_HEAD_DIM, (hd + 1) * X_HEAD_DIM)
        s = _dot_nt(q[:, sl], k_ref[:, sl])
        e = jnp.exp(s - jnp.max(s, axis=-1, keepdims=True))
        p = e / jnp.sum(e, axis=-1, keepdims=True)
        outs.append(_dot(p.astype(BF16), v_ref[:, sl]))
    o = jnp.concatenate(outs, axis=-1).astype(BF16)
    h2 = _layer_norm(DN_ALPHA * h + _dot(o, wo_ref[...]), g_ref[...], b_ref[...])
    o_ref[...] = h2
    op_ref[...] = _pack_bf16_pairs(h2)


def _xattn(h1, k, v, w_xq, w_xo, ln_g, ln_b, bsz, seq, tm=512):
    T = bsz * seq
    nt = seq // tm
    full = lambda shape: pl.BlockSpec(shape, lambda b, i: (0,) * len(shape))
    return pl.pallas_call(
        _xattn_kernel,
        grid=(bsz, nt),
        in_specs=[
            pl.BlockSpec((tm, D_MODEL), lambda b, i: (b * nt + i, 0)),
            pl.BlockSpec((MEM_LEN, D_MODEL), lambda b, i: (b, 0)),
            pl.BlockSpec((MEM_LEN, D_MODEL), lambda b, i: (b, 0)),
            full((D_MODEL, D_MODEL)), full((D_MODEL, D_MODEL)), full((1, D_MODEL)), full((1, D_MODEL)),
        ],
        out_specs=[pl.BlockSpec((tm, D_MODEL), lambda b, i: (b * nt + i, 0)),
                   pl.BlockSpec((tm, D_MODEL // 2), lambda b, i: (b * nt + i, 0))],
        out_shape=[jax.ShapeDtypeStruct((T, D_MODEL), F32), jax.ShapeDtypeStruct((T, D_MODEL // 2), jnp.uint32)],
        compiler_params=_cparams("parallel", "parallel"),
        name="xattn",
    )(h1, k, v, w_xq, w_xo, ln_g, ln_b)


def _router_kernel(h_ref, w_ref, b_ref, idx_ref, p_ref, rank_ref, cnt_ref, run_ref):
    tm = h_ref.shape[0]
    i = pl.program_id(0)

    @pl.when(i == 0)
    def _():
        run_ref[...] = jnp.zeros_like(run_ref)

    lane = lax.broadcasted_iota(jnp.int32, (tm, LANES), 1)
    lane_f = lane.astype(F32)
    neg = jnp.float32(-jnp.inf)
    logits = _dot(h_ref[...], w_ref[...], HIGHEST) + b_ref[...]
    work = jnp.where(lane < N_EXPERTS, logits, neg)
    vals, hots = [], []
    idx_out = jnp.zeros((tm, LANES), F32)
    for k in range(TOP_K):
        m = jnp.max(work, axis=-1, keepdims=True)
        first = jnp.min(jnp.where(work == m, lane_f, float(LANES)), axis=-1, keepdims=True)
        hot = lane_f == first
        vals.append(m)
        hots.append(hot)
        idx_out = jnp.where(lane == k, first, idx_out)
        work = jnp.where(hot, neg, work)
    exps = [jnp.exp(v - vals[0]) for v in vals]
    denom = exps[0] + exps[1] + exps[2] + exps[3]
    member = jnp.zeros((tm, LANES), F32)
    p_out = jnp.zeros((tm, LANES), F32)
    for k in range(TOP_K):
        member = jnp.where(hots[k], 1.0, member)
        p_out = jnp.where(lane == k, exps[k] / denom, p_out)
    row = lax.broadcasted_iota(jnp.int32, (tm, tm), 0)
    col = lax.broadcasted_iota(jnp.int32, (tm, tm), 1)
    before = (col < row).astype(BF16)
    earlier = _dot(before, member.astype(BF16)) + run_ref[...]
    rank_out = jnp.zeros((tm, LANES), F32)
    for k in range(TOP_K):
        r = jnp.sum(jnp.where(hots[k], earlier, 0.0), axis=-1, keepdims=True)
        rank_out = jnp.where(lane == k, r, rank_out)
    run = run_ref[...] + jnp.sum(member, axis=0, keepdims=True)
    run_ref[...] = run
    cnt_ref[...] = run
    idx_ref[...] = idx_out.astype(jnp.int32)
    p_ref[...] = p_out
    rank_ref[...] = rank_out.astype(jnp.int32)


def _router(h2, w_r, b_r, tm=512):
    T = h2.shape[0]
    full = lambda shape: pl.BlockSpec(shape, lambda i: (0,) * len(shape))
    tok = pl.BlockSpec((tm, LANES), lambda i: (i, 0))
    return pl.pallas_call(
        _router_kernel,
        grid=(T // tm,),
        in_specs=[pl.BlockSpec((tm, D_MODEL), lambda i: (i, 0)), full((D_MODEL, LANES)), full((1, LANES))],
        out_specs=[tok, tok, tok, full((1, LANES))],
        out_shape=[
            jax.ShapeDtypeStruct((T, LANES), jnp.int32),
            jax.ShapeDtypeStruct((T, LANES), F32),
            jax.ShapeDtypeStruct((T, LANES), jnp.int32),
            jax.ShapeDtypeStruct((1, LANES), F32),
        ],
        scratch_shapes=[pltpu.VMEM((1, LANES), F32)],
        compiler_params=_cparams("arbitrary"),
        name="router",
    )(h2, w_r, b_r)


def _dispatch_kernel(dest_ref, h_ref, xin_ref, x_ref, sem):
    del xin_ref
    td = h_ref.shape[0]

    for t in range(td):
        for k in range(TOP_K):
            d = dest_ref[t * TOP_K + k]
            pltpu.make_async_copy(h_ref.at[pl.ds(t, 1), :], x_ref.at[pl.ds(d, 1), :], sem).start(priority=k % 2)

    def wait(t, _):
        for k in range(TOP_K):
            pltpu.make_async_copy(h_ref.at[pl.ds(0, 1), :], x_ref.at[pl.ds(0, 1), :], sem).wait()
        return 0

    lax.fori_loop(0, td, wait, 0, unroll=8)


def _dispatch(dest_flat, h2_packed, n_rows, td=256):
    T, width = h2_packed.shape
    x0 = jnp.zeros((n_rows, width), h2_packed.dtype)
    return pl.pallas_call(
        _dispatch_kernel,
        grid=(T // td,),
        in_specs=[
            pl.BlockSpec((td * TOP_K,), lambda i: (i,), memory_space=pltpu.SMEM),
            pl.BlockSpec((td, width), lambda i: (i, 0)),
            pl.BlockSpec(memory_space=pl.ANY),
        ],
        out_specs=pl.BlockSpec(memory_space=pl.ANY),
        out_shape=jax.ShapeDtypeStruct((n_rows, width), h2_packed.dtype),
        scratch_shapes=[pltpu.SemaphoreType.DMA(())],
        input_output_aliases={2: 0},
        compiler_params=_cparams("arbitrary"),
        name="moe_dispatch",
    )(dest_flat, h2_packed, x0)


def _expert_kernel(be_ref, nused_ref, x_ref, wg_ref, bg_ref, wu_ref, bu_ref, wd_ref, bd_ref, y_ref,
                   wg_bf, wu_bf, wd_bf):
    blk = pl.program_id(0)
    new_expert = (blk == 0) | (be_ref[blk] != be_ref[jnp.maximum(blk - 1, 0)])

    @pl.when(new_expert & (blk < nused_ref[0]))
    def _():
        wg_bf[...] = wg_ref[0].astype(BF16)
        wu_bf[...] = wu_ref[0].astype(BF16)
        wd_bf[...] = wd_ref[0].astype(BF16)

    @pl.when(blk < nused_ref[0])
    def _():
        xb = _unpack_bf16_pairs(x_ref[...])
        g = jnp.minimum(_dot(xb, wg_bf[...]) + bg_ref[0], SWIGLU_LIMIT)
        u = jnp.clip(_dot(xb, wu_bf[...]) + bu_ref[0], -SWIGLU_LIMIT, SWIGLU_LIMIT)
        act = (u + 1.0) * g * jax.nn.sigmoid(SWIGLU_ALPHA * g)
        y_ref[...] = _dot(act.astype(BF16), wd_bf[...]) + bd_ref[0]

    @pl.when(blk >= nused_ref[0])
    def _():
        y_ref[...] = jnp.zeros_like(y_ref)


def _experts(block_e, n_used, x_rows, w_gate, b_gate, w_up, b_up, w_down, b_down):
    n_rows = x_rows.shape[0]
    bm = EXPERT_ROWS
    wspec = pl.BlockSpec((1, D_MODEL, D_MODEL), lambda i, be, nu: (be[i], 0, 0))
    bspec = pl.BlockSpec((1, 1, D_MODEL), lambda i, be, nu: (be[i], 0, 0))
    return pl.pallas_call(
        _expert_kernel,
        grid_spec=pltpu.PrefetchScalarGridSpec(
            num_scalar_prefetch=2,
            grid=(n_rows // bm,),
            in_specs=[pl.BlockSpec((bm, D_MODEL // 2), lambda i, be, nu: (i, 0)),
                      wspec, bspec, wspec, bspec, wspec, bspec],
            out_specs=pl.BlockSpec((bm, D_MODEL), lambda i, be, nu: (i, 0)),
            scratch_shapes=[pltpu.VMEM((D_MODEL, D_MODEL), BF16)] * 3,
        ),
        out_shape=jax.ShapeDtypeStruct((n_rows, D_MODEL), F32),
        compiler_params=_cparams("arbitrary"),
        name="moe_experts",
    )(block_e, n_used, x_rows, w_gate, b_gate, w_up, b_up, w_down, b_down)


def _combine_kernel(dcur_ref, dnext_ref, y_ref, p_ref, h_ref, g_ref, b_ref, o_ref, buf, sem):
    tc = h_ref.shape[0]
    i = pl.program_id(0)
    n = pl.num_programs(0)

    def issue(dref, slot):
        for t in range(tc):
            for k in range(TOP_K):
                d = dref[t * TOP_K + k]
                pltpu.make_async_copy(y_ref.at[pl.ds(d, 1), :], buf.at[slot, k, pl.ds(t, 1), :],
                                      sem.at[slot]).start(priority=k % 2)

    slot = i % 2

    @pl.when(i == 0)
    def _():
        issue(dcur_ref, 0)

    @pl.when((i + 1 < n) & (slot == 1))
    def _():
        issue(dnext_ref, 0)

    @pl.when((i + 1 < n) & (slot == 0))
    def _():
        issue(dnext_ref, 1)

    def wait_body(t, _):
        for k in range(TOP_K):
            pltpu.make_async_copy(y_ref.at[pl.ds(0, 1), :], buf.at[slot, k, pl.ds(0, 1), :], sem.at[slot]).wait()
        return 0

    lax.fori_loop(0, tc, wait_body, 0, unroll=8)

    p = p_ref[...]
    y = p[:, 0:1] * buf[slot, 0]
    for k in range(1, TOP_K):
        y = y + p[:, k:k + 1] * buf[slot, k]
    o_ref[...] = _layer_norm(DN_ALPHA * h_ref[...] + y, g_ref[...], b_ref[...])


def _combine(dest_flat, y_rows, probs, h2, ln_g, ln_b, tc=128):
    T = h2.shape[0]
    nt = T // tc
    full = lambda shape: pl.BlockSpec(shape, lambda i: (0,) * len(shape))
    return pl.pallas_call(
        _combine_kernel,
        grid=(nt,),
        in_specs=[
            pl.BlockSpec((tc * TOP_K,), lambda i: (i,), memory_space=pltpu.SMEM),
            pl.BlockSpec((tc * TOP_K,), lambda i: (jnp.minimum(i + 1, nt - 1),), memory_space=pltpu.SMEM),
            pl.BlockSpec(memory_space=pl.ANY),
            pl.BlockSpec((tc, LANES), lambda i: (i, 0)),
            pl.BlockSpec((tc, D_MODEL), lambda i: (i, 0)),
            full((1, D_MODEL)), full((1, D_MODEL)),
        ],
        out_specs=pl.BlockSpec((tc, D_MODEL), lambda i: (i, 0)),
        out_shape=jax.ShapeDtypeStruct((T, D_MODEL), F32),
        scratch_shapes=[pltpu.VMEM((2, TOP_K, tc, D_MODEL), F32), pltpu.SemaphoreType.DMA((2,))],
        compiler_params=_cparams("arbitrary"),
        name="moe_combine",
    )(dest_flat, dest_flat, y_rows, probs, h2, ln_g, ln_b)


def _pad_lanes(v, fill=0.0):
    v = v.reshape(1, -1)
    return jnp.pad(v, ((0, 0), (0, LANES - v.shape[1])), constant_values=fill)


def _mixer_stage(x2, bsz, seq, ln_in_g, ln_in_b, w_in, b_branch_gate, conv_w, conv_b, dt_bias, a_log, d_skip,
                 ssd_norm_g, w_sb, w_ssd, w_mix_out, ln1_g, ln1_b):
    n_lin = 3 * D_MODEL + SSD_INNER + SSD_CONV_CH
    w_main = jnp.concatenate([w_in[:, :n_lin], w_in[:, n_lin + SSD_HEADS:]], axis=1).astype(BF16)
    w_dt = jnp.pad(w_in[:, n_lin:n_lin + SSD_HEADS], ((0, 0), (0, LANES - SSD_HEADS))).astype(BF16)
    h, qkv, z, xbc, gate_logits, dt_raw = _inproj(x2, ln_in_g.reshape(1, -1), ln_in_b.reshape(1, -1), w_main, w_dt)

    o_sb = _stick_breaking(qkv, bsz, seq)

    xbc_act = _conv_silu(xbc, conv_w, conv_b.reshape(1, -1), bsz, seq)
    a = -jnp.exp(a_log.astype(F32))
    o_ssd = _ssd(xbc_act, z, dt_raw, _pad_lanes(dt_bias), _pad_lanes(a),
                 jnp.repeat(d_skip, SSD_HEAD_DIM).reshape(1, -1), ssd_norm_g.reshape(1, -1), bsz, seq)

    return _merge(o_sb, o_ssd, gate_logits, h, b_branch_gate.reshape(1, -1), w_sb.astype(BF16), w_ssd.astype(BF16),
                  w_mix_out.astype(BF16), ln1_g.reshape(1, -1), ln1_b.reshape(1, -1))


def _xattn_stage(h1, mem2, bsz, seq, w_xq, w_xk, w_xv, w_xo, ln2_g, ln2_b):
    k, v = _kv_proj(mem2, w_xk.astype(BF16), w_xv.astype(BF16))
    return _xattn(h1, k, v, w_xq.astype(BF16), w_xo.astype(BF16), ln2_g.reshape(1, -1), ln2_b.reshape(1, -1),
                  bsz, seq)


def _moe_stage(h2, h2_packed, w_router, b_router, w_e_gate, b_e_gate, w_e_up, b_e_up, w_e_down, b_e_down, ln3_g,
               ln3_b):
    T = h2.shape[0]
    bm = EXPERT_ROWS
    w_r = jnp.pad(w_router, ((0, 0), (0, LANES - N_EXPERTS)))
    idx_p, probs, rank_p, counts_p = _router(h2, w_r, _pad_lanes(b_router))
    idx = idx_p[:, :TOP_K]
    rank = rank_p[:, :TOP_K]
    counts = counts_p[0, :N_EXPERTS].astype(jnp.int32)
    padded = (counts + bm - 1) // bm * bm
    end_padded = jnp.cumsum(padded)
    start_padded = end_padded - padded
    onehot = idx[:, :, None] == jnp.arange(N_EXPERTS, dtype=jnp.int32)[None, None, :]
    dest = jnp.sum(jnp.where(onehot, start_padded[None, None, :], 0), axis=-1) + rank
    dest_flat = dest.reshape(-1).astype(jnp.int32)
    n_blocks = -(-(T * TOP_K + N_EXPERTS * (bm - 1)) // bm)
    n_rows = n_blocks * bm
    blk_start = jnp.arange(n_blocks, dtype=jnp.int32) * bm
    block_e = jnp.minimum(jnp.sum(blk_start[:, None] >= end_padded[None, :], axis=-1), N_EXPERTS - 1).astype(jnp.int32)
    n_used = (end_padded[-1:] // bm).astype(jnp.int32)

    x_rows = _dispatch(dest_flat, h2_packed, n_rows)
    y_rows = _experts(block_e, n_used, x_rows, w_e_gate, b_e_gate[:, None, :], w_e_up, b_e_up[:, None, :],
                      w_e_down, b_e_down[:, None, :])
    return _combine(dest_flat, y_rows, probs, h2, ln3_g.reshape(1, -1), ln3_b.reshape(1, -1))


def kernel(x, mem, ln_in_g, ln_in_b, w_in, b_branch_gate, conv_w, conv_b, dt_bias, a_log, d_skip, ssd_norm_g, w_sb,
           w_ssd, w_mix_out, ln1_g, ln1_b, w_xq, w_xk, w_xv, w_xo, ln2_g, ln2_b, w_router, b_router, w_e_gate,
           b_e_gate, w_e_up, b_e_up, w_e_down, b_e_down, ln3_g, ln3_b):
    bsz, seq, _ = x.shape
    depth = w_in.shape[0]
    x2 = x.reshape(bsz * seq, D_MODEL)
    mem2 = mem.reshape(bsz * mem.shape[1], D_MODEL)
    assert depth == 1, "the entry LayerNorm is fused into the single layer's input projection"
    l = 0
    h1 = _mixer_stage(x2, bsz, seq, ln_in_g, ln_in_b, w_in[l], b_branch_gate[l], conv_w[l], conv_b[l], dt_bias[l],
                      a_log[l], d_skip[l], ssd_norm_g[l], w_sb[l], w_ssd[l], w_mix_out[l], ln1_g[l], ln1_b[l])
    h2, h2_packed = _xattn_stage(h1, mem2, bsz, seq, w_xq[l], w_xk[l], w_xv[l], w_xo[l], ln2_g[l], ln2_b[l])
    h = _moe_stage(h2, h2_packed, w_router[l], b_router[l], w_e_gate[l], b_e_gate[l], w_e_up[l], b_e_up[l], w_e_down[l],
                   b_e_down[l], ln3_g[l], ln3_b[l])
    return h.reshape(bsz, seq, D_MODEL)
```

```python
import functools

import jax
import jax.numpy as jnp
from jax import lax
from jax.experimental import pallas as pl
from jax.experimental.pallas import tpu as pltpu

F32 = jnp.float32
BF16 = jnp.bfloat16
HIGHEST = lax.Precision.HIGHEST

D_MODEL = 1024
LN_EPS = 1e-5
DN_ALPHA = 2.0 ** 0.25
LANES = 128

SB_HEADS = 16
SB_HEAD_DIM = 64
SB_TILE = 256
SB_LANES = 256
SB_GROUP = SB_LANES // SB_HEAD_DIM
LOG2E = 1.4426950408889634
SB_COLS = 2
SB_SKIP_BITS = 152.0

SSD_INNER = 2048
SSD_HEADS = 32
SSD_HEAD_DIM = 64
SSD_GROUPS = 4
SSD_GROUP_WIDTH = SSD_INNER // SSD_GROUPS
SSD_STATE = 128
SSD_CONV = 4
SSD_CONV_CH = SSD_INNER + 2 * SSD_GROUPS * SSD_STATE
SSD_CHUNK = 128

X_HEADS = 4
X_HEAD_DIM = 256
MEM_LEN = 256

N_EXPERTS = 32
TOP_K = 4
SWIGLU_LIMIT = 7.0
SWIGLU_ALPHA = 1.702
EXPERT_ROWS = 512

VMEM_LIMIT = 52 * 1024 * 1024


def _cparams(*sem):
    return pltpu.CompilerParams(dimension_semantics=sem, vmem_limit_bytes=VMEM_LIMIT)


def _layer_norm(x, g, b):
    mu = jnp.mean(x, axis=-1, keepdims=True)
    xc = x - mu
    var = jnp.mean(xc * xc, axis=-1, keepdims=True)
    return xc * lax.rsqrt(var + LN_EPS) * g + b


def _dot(a, b, precision=None):
    return jnp.dot(a, b, preferred_element_type=F32, precision=precision)


def _dot_nt(a, b, precision=None):
    return lax.dot_general(a, b, (((1,), (1,)), ((), ())), preferred_element_type=F32, precision=precision)


_IN_TN = 1024
_N_QKV, _N_Z, _N_XBC, _N_GATE = (3 * D_MODEL // _IN_TN, SSD_INNER // _IN_TN, SSD_CONV_CH // _IN_TN,
                                 2 * D_MODEL // _IN_TN)
_QKV_SPLIT = D_MODEL // _IN_TN


def _inproj_kernel(x_ref, g_ref, b_ref, w_ref, wdt_ref, h_ref, qkv_ref, z_ref, xbc_ref, gate_ref, dt_ref, hb_ref):
    j = pl.program_id(1)

    @pl.when(j == 0)
    def _():
        h = _layer_norm(x_ref[...], g_ref[...], b_ref[...])
        h_ref[...] = h
        hb = h.astype(BF16)
        hb_ref[...] = hb
        dt_ref[...] = _dot(hb, wdt_ref[...])

    @pl.when(j < _N_QKV)
    def _():
        for c in range(_IN_TN // SB_LANES):
            sl = slice(c * SB_LANES, (c + 1) * SB_LANES)
            qkv_ref[0, c] = _dot(hb_ref[...], w_ref[:, sl]).astype(BF16)

    @pl.when((j >= _N_QKV) & (j < _N_QKV + _N_Z))
    def _():
        z_ref[...] = _dot(hb_ref[...], w_ref[...]).astype(BF16)

    @pl.when((j >= _N_QKV + _N_Z) & (j < _N_QKV + _N_Z + _N_XBC))
    def _():
        xbc_ref[...] = _dot(hb_ref[...], w_ref[...]).astype(BF16)

    @pl.when(j >= _N_QKV + _N_Z + _N_XBC)
    def _():
        gate_ref[...] = _dot(hb_ref[...], w_ref[...]).astype(BF16)


def _inproj(x2, ln_g, ln_b, w_main, w_dt, tm=1024):
    T = x2.shape[0]
    tm = min(tm, T)
    tn = _IN_TN
    nj = _N_QKV + _N_Z + _N_XBC + _N_GATE
    o_z, o_xbc, o_gate = _N_QKV, _N_QKV + _N_Z, _N_QKV + _N_Z + _N_XBC

    def qkv_map(i, j):
        jq = jnp.minimum(j, _N_QKV - 1)
        return (jq // _QKV_SPLIT, jq % _QKV_SPLIT, i, 0)

    return pl.pallas_call(
        _inproj_kernel,
        grid=(T // tm, nj),
        in_specs=[
            pl.BlockSpec((tm, D_MODEL), lambda i, j: (i, 0)),
            pl.BlockSpec((1, D_MODEL), lambda i, j: (0, 0)),
            pl.BlockSpec((1, D_MODEL), lambda i, j: (0, 0)),
            pl.BlockSpec((D_MODEL, tn), lambda i, j: (0, j)),
            pl.BlockSpec((D_MODEL, LANES), lambda i, j: (0, 0)),
        ],
        out_specs=[
            pl.BlockSpec((tm, D_MODEL), lambda i, j: (i, 0)),
            pl.BlockSpec((1, tn // SB_LANES, tm, SB_LANES), qkv_map),
            pl.BlockSpec((tm, tn), lambda i, j: (i, jnp.clip(j - o_z, 0, _N_Z - 1))),
            pl.BlockSpec((tm, tn), lambda i, j: (i, jnp.clip(j - o_xbc, 0, _N_XBC - 1))),
            pl.BlockSpec((tm, tn), lambda i, j: (i, jnp.clip(j - o_gate, 0, _N_GATE - 1))),
            pl.BlockSpec((tm, LANES), lambda i, j: (i, 0)),
        ],
        out_shape=[
            jax.ShapeDtypeStruct((T, D_MODEL), F32),
            jax.ShapeDtypeStruct((3, D_MODEL // SB_LANES, T, SB_LANES), BF16),
            jax.ShapeDtypeStruct((T, SSD_INNER), BF16),
            jax.ShapeDtypeStruct((T, SSD_CONV_CH), BF16),
            jax.ShapeDtypeStruct((T, 2 * D_MODEL), BF16),
            jax.ShapeDtypeStruct((T, LANES), F32),
        ],
        scratch_shapes=[pltpu.VMEM((tm, D_MODEL), BF16)],
        compiler_params=_cparams("parallel", "arbitrary"),
        name="inproj",
    )(x2, ln_g, ln_b, w_main, w_dt)


def _sb_kernel(q_ref, k_ref, v_ref, o_ref):
    t = SB_TILE
    G = SB_GROUP
    NC = q_ref.shape[1]
    i = pl.program_id(2)
    lane = lax.broadcasted_iota(jnp.int32, (t, SB_LANES), 1)
    head_lanes = [(lane >= h * SB_HEAD_DIM) & (lane < (h + 1) * SB_HEAD_DIM) for h in range(G)]
    q_stacks = []
    for n in range(NC):
        qs = q_ref[0, n] * BF16(SB_HEAD_DIM ** -0.5)
        zero = jnp.zeros_like(qs)
        q_stacks.append(jnp.concatenate([jnp.where(m, qs, zero) for m in head_lanes], axis=0))
    row = lax.broadcasted_iota(jnp.int32, (G * t, t), 0) & (t - 1)
    col = lax.broadcasted_iota(jnp.int32, (G * t, t), 1)
    causal = col < row
    krow = lax.broadcasted_iota(jnp.int32, (2 * t, t), 0) & (t - 1)
    kcol = lax.broadcasted_iota(jnp.int32, (2 * t, t), 1)
    later2 = (krow > kcol).astype(BF16)

    def block(n, jb, c, acc, diag):
        k = k_ref[0, n, pl.ds(pl.multiple_of(jb * t, t), t), :]
        v = v_ref[0, n, pl.ds(pl.multiple_of(jb * t, t), t), :]
        vzero = jnp.zeros_like(v)
        v_stack = jnp.concatenate([jnp.where(m, v, vzero) for m in head_lanes], axis=0)
        z2 = _dot_nt(q_stacks[n], k) * LOG2E
        neg_abs = lax.bitcast_convert_type(lax.bitcast_convert_type(z2, jnp.uint32) | jnp.uint32(0x80000000), F32)
        sp2 = jnp.maximum(z2, 0.0) + jnp.log2(1.0 + jnp.exp2(neg_abs))
        spm = jnp.where(causal, sp2, 0.0) if diag else sp2
        hi = spm.astype(BF16)
        lo = (spm - hi.astype(F32)).astype(BF16)
        after = _dot(jnp.concatenate([hi, lo], axis=1), later2)
        w = jnp.exp2(z2 - sp2 - after - c)
        if diag:
            w = jnp.where(causal, w, 0.0)
        wb = w.astype(BF16)
        w_cat = jnp.concatenate([wb[h * t:(h + 1) * t] for h in range(G)], axis=1)
        return c + jnp.sum(spm, axis=1, keepdims=True), acc + _dot(w_cat, v_stack)

    def blocks(jb, cs, accs, diag):
        out = [block(n, jb, cs[n], accs[n], diag) for n in range(NC)]
        return tuple(o[0] for o in out), tuple(o[1] for o in out)

    def min_decay(cs):
        m = jnp.min(cs[0])
        for c in cs[1:]:
            m = jnp.minimum(m, jnp.min(c))
        return m

    cs = tuple(jnp.zeros((G * t, 1), F32) for _ in range(NC))
    accs = tuple(jnp.zeros((t, SB_LANES), F32) for _ in range(NC))
    cs, accs = blocks(i, cs, accs, True)

    def cond(carry):
        jb, m, _, _ = carry
        return (jb >= 0) & (m < SB_SKIP_BITS)

    def body(carry):
        jb, _, cs, accs = carry
        cs, accs = blocks(jb, cs, accs, False)
        return jb - 1, min_decay(cs), cs, accs

    _, _, _, accs = lax.while_loop(cond, body, (i - 1, min_decay(cs), cs, accs))
    for n in range(NC):
        o_ref[:, n * SB_LANES:(n + 1) * SB_LANES] = accs[n].astype(BF16)


def _stick_breaking(qkv, bsz, seq):
    t = SB_TILE
    nq = seq // t
    T = bsz * seq
    nc = SB_COLS
    return pl.pallas_call(
        _sb_kernel,
        grid=(bsz, SB_HEADS // (SB_GROUP * nc), nq),
        in_specs=[
            pl.BlockSpec((1, nc, t, SB_LANES), lambda b, c, i: (0, c, b * nq + i, 0)),
            pl.BlockSpec((1, nc, seq, SB_LANES), lambda b, c, i: (1, c, b, 0)),
            pl.BlockSpec((1, nc, seq, SB_LANES), lambda b, c, i: (2, c, b, 0)),
        ],
        out_specs=pl.BlockSpec((t, nc * SB_LANES), lambda b, c, i: (b * nq + i, c)),
        out_shape=jax.ShapeDtypeStruct((T, SB_HEADS * SB_HEAD_DIM), BF16),
        compiler_params=_cparams("parallel", "parallel", "arbitrary"),
        name="stick_breaking",
    )(qkv, qkv, qkv)


_CONV_HALO = 8


def _conv_kernel(u_ref, w_ref, b_ref, o_ref, ext_ref):
    ts = u_ref.shape[0]
    s = pl.program_id(1)

    @pl.when(s == 0)
    def _():
        ext_ref[0:_CONV_HALO, :] = jnp.zeros((_CONV_HALO, SSD_CONV_CH), F32)

    @pl.when(s > 0)
    def _():
        ext_ref[0:_CONV_HALO, :] = ext_ref[ts:ts + _CONV_HALO, :]

    ext_ref[_CONV_HALO:_CONV_HALO + ts, :] = u_ref[...].astype(F32)
    cw = 512
    for c0 in range(0, SSD_CONV_CH, cw):
        ext = ext_ref[:, c0:c0 + cw]
        acc = b_ref[:, c0:c0 + cw] + w_ref[SSD_CONV - 1, 0:1, c0:c0 + cw] * ext[_CONV_HALO:]
        for d in range(1, SSD_CONV):
            rolled = pltpu.roll(ext, d, axis=0)
            acc = acc + w_ref[SSD_CONV - 1 - d, 0:1, c0:c0 + cw] * rolled[_CONV_HALO:]
        o_ref[:, c0:c0 + cw] = acc * jax.nn.sigmoid(acc)


def _conv_silu(xbc, conv_w, conv_b, bsz, seq, ts=512):
    T = bsz * seq
    ns = seq // ts
    conv_w = jnp.broadcast_to(conv_w[:, None, :], (SSD_CONV, 8, SSD_CONV_CH))
    return pl.pallas_call(
        _conv_kernel,
        grid=(bsz, ns),
        in_specs=[
            pl.BlockSpec((ts, SSD_CONV_CH), lambda b, s: (b * ns + s, 0)),
            pl.BlockSpec((SSD_CONV, 8, SSD_CONV_CH), lambda b, s: (0, 0, 0)),
            pl.BlockSpec((1, SSD_CONV_CH), lambda b, s: (0, 0)),
        ],
        out_specs=pl.BlockSpec((ts, SSD_CONV_CH), lambda b, s: (b * ns + s, 0)),
        out_shape=jax.ShapeDtypeStruct((T, SSD_CONV_CH), F32),
        scratch_shapes=[pltpu.VMEM((ts + _CONV_HALO, SSD_CONV_CH), F32)],
        compiler_params=_cparams("parallel", "arbitrary"),
        name="conv_silu",
    )(xbc, conv_w, conv_b)


def _ssd_kernel(xs_ref, bm_ref, cm_ref, z_ref, dtr_ref, dtb_ref, a128_ref, dexp_ref, ng_ref, e_ref,
                o_ref, state_ref, y_ref):
    L = SSD_CHUNK
    c = pl.program_id(1)

    @pl.when(c == 0)
    def _():
        state_ref[...] = jnp.zeros_like(state_ref)

    row = lax.broadcasted_iota(jnp.int32, (L, L), 0)
    col = lax.broadcasted_iota(jnp.int32, (L, L), 1)
    incl = (col <= row).astype(F32)
    eye = (col == row).astype(F32)
    lower = col <= row
    lane = lax.broadcasted_iota(jnp.int32, (L, LANES), 1)

    def expand(v):
        v = jnp.where(lane < SSD_HEADS, v, 0.0)
        hi = v.astype(BF16).astype(F32)
        r1 = v - hi
        mid = r1.astype(BF16).astype(F32)
        lo = (r1 - mid).astype(BF16).astype(F32)
        pieces = hi + pltpu.roll(mid, SSD_HEADS, axis=1) + pltpu.roll(lo, 2 * SSD_HEADS, axis=1)
        return _dot(pieces.astype(BF16), e_ref[...])

    dt = jax.nn.softplus(dtr_ref[...] + dtb_ref[...])
    dt_exp = expand(dt)
    cs_col = _dot(incl, dt * a128_ref[...], HIGHEST)
    cs_exp = expand(cs_col)
    cs_row = _dot_nt(eye, cs_col, HIGHEST)

    xs = xs_ref[...]
    xdt = xs * dt_exp
    xdt_b = xdt.astype(BF16)
    cs_last = cs_exp[L - 1:L, :]
    xw_b = (xdt * jnp.exp(cs_last - cs_exp)).astype(BF16)
    grow = jnp.exp(cs_exp)
    grow_last = jnp.exp(cs_last)

    gw = SSD_GROUP_WIDTH
    heads_per_group = SSD_HEADS // SSD_GROUPS
    for g in range(SSD_GROUPS):
        bg = bm_ref[:, g * SSD_STATE:(g + 1) * SSD_STATE]
        cg = cm_ref[:, g * SSD_STATE:(g + 1) * SSD_STATE].astype(BF16)
        cb = _dot_nt(cg, bg.astype(BF16))
        for pair in range(heads_per_group // 2):
            c0 = g * gw + pair * LANES
            x_pair = xdt_b[:, c0:c0 + LANES]
            x_zero = jnp.zeros_like(x_pair)
            x_stack = jnp.concatenate([jnp.where(lane < SSD_HEAD_DIM, x_pair, x_zero),
                                       jnp.where(lane < SSD_HEAD_DIM, x_zero, x_pair)], axis=0)
            m_pair = []
            for sub in range(2):
                hd = g * heads_per_group + pair * 2 + sub
                diff = cs_col[:, hd:hd + 1] - cs_row[hd:hd + 1, :]
                seg = jnp.where(lower, jnp.exp(jnp.minimum(diff, 0.0)), 0.0)
                m_pair.append((cb * seg).astype(BF16))
            y_ref[:, c0:c0 + LANES] = _dot(jnp.concatenate(m_pair, axis=1), x_stack)
        st = state_ref[g]
        y_ref[:, g * gw:(g + 1) * gw] += _dot(cg, st.astype(BF16)) * grow[:, g * gw:(g + 1) * gw]
        state_ref[g] = st * grow_last[:, g * gw:(g + 1) * gw] + _dot(bg.T.astype(BF16), xw_b[:, g * gw:(g + 1) * gw])

    z = z_ref[...].astype(F32)
    y = (y_ref[...] + dexp_ref[...] * xs) * (z * jax.nn.sigmoid(z))
    for g in range(SSD_GROUPS):
        yg = y[:, g * gw:(g + 1) * gw]
        ms = jnp.mean(yg * yg, axis=-1, keepdims=True)
        o_ref[:, g * gw:(g + 1) * gw] = (yg * lax.rsqrt(ms + LN_EPS) * ng_ref[:, g * gw:(g + 1) * gw]).astype(BF16)


def _ssd(xbc_act, z, dt_raw, dt_bias128, a128, d_exp, norm_g, bsz, seq):
    L = SSD_CHUNK
    T = bsz * seq
    nc = seq // L
    nb = SSD_INNER // SSD_GROUP_WIDTH
    rows = jnp.arange(LANES)
    expand = ((rows[:, None] < 3 * SSD_HEADS)
              & ((rows[:, None] % SSD_HEADS) == (jnp.arange(SSD_INNER)[None, :] // SSD_HEAD_DIM))).astype(BF16)
    full = lambda shape: pl.BlockSpec(shape, lambda b, c: (0,) * len(shape))
    return pl.pallas_call(
        _ssd_kernel,
        grid=(bsz, nc),
        in_specs=[
            pl.BlockSpec((L, SSD_INNER), lambda b, c: (b * nc + c, 0)),
            pl.BlockSpec((L, SSD_GROUPS * SSD_STATE), lambda b, c: (b * nc + c, nb)),
            pl.BlockSpec((L, SSD_GROUPS * SSD_STATE), lambda b, c: (b * nc + c, nb + 1)),
            pl.BlockSpec((L, SSD_INNER), lambda b, c: (b * nc + c, 0)),
            pl.BlockSpec((L, LANES), lambda b, c: (b * nc + c, 0)),
            full((1, LANES)), full((1, LANES)), full((1, SSD_INNER)), full((1, SSD_INNER)),
            full((LANES, SSD_INNER)),
        ],
        out_specs=pl.BlockSpec((L, SSD_INNER), lambda b, c: (b * nc + c, 0)),
        out_shape=jax.ShapeDtypeStruct((T, SSD_INNER), BF16),
        scratch_shapes=[
            pltpu.VMEM((SSD_GROUPS, SSD_STATE, SSD_GROUP_WIDTH), F32),
            pltpu.VMEM((L, SSD_INNER), F32),
        ],
        compiler_params=_cparams("parallel", "arbitrary"),
        name="ssd_scan",
    )(xbc_act, xbc_act, xbc_act, z, dt_raw, dt_bias128, a128, d_exp, norm_g, expand)


def _merge_kernel(osb_ref, ossd_ref, gl_ref, h_ref, bg_ref, wsb_ref, wssd_ref, wmix_ref, g_ref, b_ref, o_ref):
    gates = jax.nn.sigmoid(gl_ref[...].astype(F32) + bg_ref[...])
    merged = (gates[:, :D_MODEL] * _dot(osb_ref[...], wsb_ref[...])
              + gates[:, D_MODEL:] * _dot(ossd_ref[...], wssd_ref[...]))
    mix = _dot(merged.astype(BF16), wmix_ref[...])
    o_ref[...] = _layer_norm(DN_ALPHA * h_ref[...] + mix, g_ref[...], b_ref[...])


def _merge(o_sb, o_ssd, gate_logits, h, b_gate, w_sb, w_ssd, w_mix, ln_g, ln_b, tm=512):
    T = h.shape[0]
    full = lambda shape: pl.BlockSpec(shape, lambda i: (0,) * len(shape))
    return pl.pallas_call(
        _merge_kernel,
        grid=(T // tm,),
        in_specs=[
            pl.BlockSpec((tm, D_MODEL), lambda i: (i, 0)),
            pl.BlockSpec((tm, SSD_INNER), lambda i: (i, 0)),
            pl.BlockSpec((tm, 2 * D_MODEL), lambda i: (i, 0)),
            pl.BlockSpec((tm, D_MODEL), lambda i: (i, 0)),
            full((1, 2 * D_MODEL)), full((D_MODEL, D_MODEL)), full((SSD_INNER, D_MODEL)), full((D_MODEL, D_MODEL)),
            full((1, D_MODEL)), full((1, D_MODEL)),
        ],
        out_specs=pl.BlockSpec((tm, D_MODEL), lambda i: (i, 0)),
        out_shape=jax.ShapeDtypeStruct((T, D_MODEL), F32),
        compiler_params=_cparams("parallel"),
        name="merge",
    )(o_sb, o_ssd, gate_logits, h, b_gate, w_sb, w_ssd, w_mix, ln_g, ln_b)


def _kv_kernel(m_ref, wk_ref, wv_ref, k_ref, v_ref):
    mb = m_ref[...].astype(BF16)
    k_ref[...] = _dot(mb, wk_ref[...]).astype(BF16)
    v_ref[...] = _dot(mb, wv_ref[...]).astype(BF16)


def _kv_proj(mem2, w_xk, w_xv, tm=512):
    M = mem2.shape[0]
    full = lambda shape: pl.BlockSpec(shape, lambda i: (0,) * len(shape))
    return pl.pallas_call(
        _kv_kernel,
        grid=(M // tm,),
        in_specs=[pl.BlockSpec((tm, D_MODEL), lambda i: (i, 0)), full((D_MODEL, D_MODEL)), full((D_MODEL, D_MODEL))],
        out_specs=[pl.BlockSpec((tm, D_MODEL), lambda i: (i, 0))] * 2,
        out_shape=[jax.ShapeDtypeStruct((M, D_MODEL), BF16)] * 2,
        compiler_params=_cparams("parallel"),
        name="kv_proj",
    )(mem2, w_xk, w_xv)


def _pack_bf16_pairs(x):
    bits = lax.bitcast_convert_type(x.astype(BF16).astype(F32), jnp.uint32)
    half = x.shape[1] // 2
    return (bits[:, :half] >> 16) | (bits[:, half:] & jnp.uint32(0xFFFF0000))


def _unpack_bf16_pairs(u):
    lo = lax.bitcast_convert_type(u << 16, F32)
    hi = lax.bitcast_convert_type(u & jnp.uint32(0xFFFF0000), F32)
    return jnp.concatenate([lo, hi], axis=1).astype(BF16)


def _xattn_kernel(h_ref, k_ref, v_ref, wq_ref, wo_ref, g_ref, b_ref, o_ref, op_ref):
    h = h_ref[...]
    q = (_dot(h.astype(BF16), wq_ref[...]) * (X_HEAD_DIM ** -0.5)).astype(BF16)
    outs = []
    for hd in range(X_HEADS):
        sl = slice(hd * X_HEAD_DIM, (hd + 1) * X_HEAD_DIM)
        s = _dot_nt(q[:, sl], k_ref[:, sl])
        e = jnp.exp(s - jnp.max(s, axis=-1, keepdims=True))
        p = e / jnp.sum(e, axis=-1, keepdims=True)
        outs.append(_dot(p.astype(BF16), v_ref[:, sl]))
    o = jnp.concatenate(outs, axis=-1).astype(BF16)
    h2 = _layer_norm(DN_ALPHA * h + _dot(o, wo_ref[...]), g_ref[...], b_ref[...])
    o_ref[...] = h2
    op_ref[...] = _pack_bf16_pairs(h2)


def _xattn(h1, k, v, w_xq, w_xo, ln_g, ln_b, bsz, seq, tm=512):
    T = bsz * seq
    nt = seq // tm
    full = lambda shape: pl.BlockSpec(shape, lambda b, i: (0,) * len(shape))
    return pl.pallas_call(
        _xattn_kernel,
        grid=(bsz, nt),
        in_specs=[
            pl.BlockSpec((tm, D_MODEL), lambda b, i: (b * nt + i, 0)),
            pl.BlockSpec((MEM_LEN, D_MODEL), lambda b, i: (b, 0)),
            pl.BlockSpec((MEM_LEN, D_MODEL), lambda b, i: (b, 0)),
            full((D_MODEL, D_MODEL)), full((D_MODEL, D_MODEL)), full((1, D_MODEL)), full((1, D_MODEL)),
        ],
        out_specs=[pl.BlockSpec((tm, D_MODEL), lambda b, i: (b * nt + i, 0)),
                   pl.BlockSpec((tm, D_MODEL // 2), lambda b, i: (b * nt + i, 0))],
        out_shape=[jax.ShapeDtypeStruct((T, D_MODEL), F32), jax.ShapeDtypeStruct((T, D_MODEL // 2), jnp.uint32)],
        compiler_params=_cparams("parallel", "parallel"),
        name="xattn",
    )(h1, k, v, w_xq, w_xo, ln_g, ln_b)


def _router_kernel(h_ref, w_ref, b_ref, idx_ref, p_ref, rank_ref, cnt_ref, run_ref):
    tm = h_ref.shape[0]
    i = pl.program_id(0)

    @pl.when(i == 0)
    def _():
        run_ref[...] = jnp.zeros_like(run_ref)

    lane = lax.broadcasted_iota(jnp.int32, (tm, LANES), 1)
    lane_f = lane.astype(F32)
    neg = jnp.float32(-jnp.inf)
    logits = _dot(h_ref[...], w_ref[...], HIGHEST) + b_ref[...]
    work = jnp.where(lane < N_EXPERTS, logits, neg)
    vals, hots = [], []
    idx_out = jnp.zeros((tm, LANES), F32)
    for k in range(TOP_K):
        m = jnp.max(work, axis=-1, keepdims=True)
        first = jnp.min(jnp.where(work == m, lane_f, float(LANES)), axis=-1, keepdims=True)
        hot = lane_f == first
        vals.append(m)
        hots.append(hot)
        idx_out = jnp.where(lane == k, first, idx_out)
        work = jnp.where(hot, neg, work)
    exps = [jnp.exp(v - vals[0]) for v in vals]
    denom = exps[0] + exps[1] + exps[2] + exps[3]
    member = jnp.zeros((tm, LANES), F32)
    p_out = jnp.zeros((tm, LANES), F32)
    for k in range(TOP_K):
        member = jnp.where(hots[k], 1.0, member)
        p_out = jnp.where(lane == k, exps[k] / denom, p_out)
    row = lax.broadcasted_iota(jnp.int32, (tm, tm), 0)
    col = lax.broadcasted_iota(jnp.int32, (tm, tm), 1)
    before = (col < row).astype(BF16)
    earlier = _dot(before, member.astype(BF16)) + run_ref[...]
    rank_out = jnp.zeros((tm, LANES), F32)
    for k in range(TOP_K):
        r = jnp.sum(jnp.where(hots[k], earlier, 0.0), axis=-1, keepdims=True)
        rank_out = jnp.where(lane == k, r, rank_out)
    run = run_ref[...] + jnp.sum(member, axis=0, keepdims=True)
    run_ref[...] = run
    cnt_ref[...] = run
    idx_ref[...] = idx_out.astype(jnp.int32)
    p_ref[...] = p_out
    rank_ref[...] = rank_out.astype(jnp.int32)


def _router(h2, w_r, b_r, tm=512):
    T = h2.shape[0]
    full = lambda shape: pl.BlockSpec(shape, lambda i: (0,) * len(shape))
    tok = pl.BlockSpec((tm, LANES), lambda i: (i, 0))
    return pl.pallas_call(
        _router_kernel,
        grid=(T // tm,),
        in_specs=[pl.BlockSpec((tm, D_MODEL), lambda i: (i, 0)), full((D_MODEL, LANES)), full((1, LANES))],
        out_specs=[tok, tok, tok, full((1, LANES))],
        out_shape=[
            jax.ShapeDtypeStruct((T, LANES), jnp.int32),
            jax.ShapeDtypeStruct((T, LANES), F32),
            jax.ShapeDtypeStruct((T, LANES), jnp.int32),
            jax.ShapeDtypeStruct((1, LANES), F32),
        ],
        scratch_shapes=[pltpu.VMEM((1, LANES), F32)],
        compiler_params=_cparams("arbitrary"),
        name="router",
    )(h2, w_r, b_r)


def _dispatch_kernel(dest_ref, h_ref, xin_ref, x_ref, sem):
    del xin_ref
    td = h_ref.shape[0]

    for t in range(td):
        for k in range(TOP_K):
            d = dest_ref[t * TOP_K + k]
            pltpu.make_async_copy(h_ref.at[pl.ds(t, 1), :], x_ref.at[pl.ds(d, 1), :], sem).start(priority=k % 2)

    def wait(t, _):
        for k in range(TOP_K):
            pltpu.make_async_copy(h_ref.at[pl.ds(0, 1), :], x_ref.at[pl.ds(0, 1), :], sem).wait()
        return 0

    lax.fori_loop(0, td, wait, 0, unroll=8)


def _dispatch(dest_flat, h2_packed, n_rows, td=256):
    T, width = h2_packed.shape
    x0 = jnp.zeros((n_rows, width), h2_packed.dtype)
    return pl.pallas_call(
        _dispatch_kernel,
        grid=(T // td,),
        in_specs=[
            pl.BlockSpec((td * TOP_K,), lambda i: (i,), memory_space=pltpu.SMEM),
            pl.BlockSpec((td, width), lambda i: (i, 0)),
            pl.BlockSpec(memory_space=pl.ANY),
        ],
        out_specs=pl.BlockSpec(memory_space=pl.ANY),
        out_shape=jax.ShapeDtypeStruct((n_rows, width), h2_packed.dtype),
        scratch_shapes=[pltpu.SemaphoreType.DMA(())],
        input_output_aliases={2: 0},
        compiler_params=_cparams("arbitrary"),
        name="moe_dispatch",
    )(dest_flat, h2_packed, x0)


def _expert_kernel(be_ref, nused_ref, x_ref, wg_ref, bg_ref, wu_ref, bu_ref, wd_ref, bd_ref, y_ref,
                   wg_bf, wu_bf, wd_bf):
    blk = pl.program_id(0)
    new_expert = (blk == 0) | (be_ref[blk] != be_ref[jnp.maximum(blk - 1, 0)])

    @pl.when(new_expert & (blk < nused_ref[0]))
    def _():
        wg_bf[...] = wg_ref[0].astype(BF16)
        wu_bf[...] = wu_ref[0].astype(BF16)
        wd_bf[...] = wd_ref[0].astype(BF16)

    @pl.when(blk < nused_ref[0])
    def _():
        xb = _unpack_bf16_pairs(x_ref[...])
        g = jnp.minimum(_dot(xb, wg_bf[...]) + bg_ref[0], SWIGLU_LIMIT)
        u = jnp.clip(_dot(xb, wu_bf[...]) + bu_ref[0], -SWIGLU_LIMIT, SWIGLU_LIMIT)
        act = (u + 1.0) * g * jax.nn.sigmoid(SWIGLU_ALPHA * g)
        y_ref[...] = _dot(act.astype(BF16), wd_bf[...]) + bd_ref[0]

    @pl.when(blk >= nused_ref[0])
    def _():
        y_ref[...] = jnp.zeros_like(y_ref)


def _experts(block_e, n_used, x_rows, w_gate, b_gate, w_up, b_up, w_down, b_down):
    n_rows = x_rows.shape[0]
    bm = EXPERT_ROWS
    wspec = pl.BlockSpec((1, D_MODEL, D_MODEL), lambda i, be, nu: (be[i], 0, 0))
    bspec = pl.BlockSpec((1, 1, D_MODEL), lambda i, be, nu: (be[i], 0, 0))
    return pl.pallas_call(
        _expert_kernel,
        grid_spec=pltpu.PrefetchScalarGridSpec(
            num_scalar_prefetch=2,
            grid=(n_rows // bm,),
            in_specs=[pl.BlockSpec((bm, D_MODEL // 2), lambda i, be, nu: (i, 0)),
                      wspec, bspec, wspec, bspec, wspec, bspec],
            out_specs=pl.BlockSpec((bm, D_MODEL), lambda i, be, nu: (i, 0)),
            scratch_shapes=[pltpu.VMEM((D_MODEL, D_MODEL), BF16)] * 3,
        ),
        out_shape=jax.ShapeDtypeStruct((n_rows, D_MODEL), F32),
        compiler_params=_cparams("arbitrary"),
        name="moe_experts",
    )(block_e, n_used, x_rows, w_gate, b_gate, w_up, b_up, w_down, b_down)


def _combine_kernel(dcur_ref, dnext_ref, y_ref, p_ref, h_ref, g_ref, b_ref, o_ref, buf, sem):
    tc = h_ref.shape[0]
    i = pl.program_id(0)
    n = pl.num_programs(0)

    def issue(dref, slot):
        for t in range(tc):
            for k in range(TOP_K):
                d = dref[t * TOP_K + k]
                pltpu.make_async_copy(y_ref.at[pl.ds(d, 1), :], buf.at[slot, k, pl.ds(t, 1), :],
                                      sem.at[slot]).start(priority=k % 2)

    slot = i % 2

    @pl.when(i == 0)
    def _():
        issue(dcur_ref, 0)

    @pl.when((i + 1 < n) & (slot == 1))
    def _():
        issue(dnext_ref, 0)

    @pl.when((i + 1 < n) & (slot == 0))
    def _():
        issue(dnext_ref, 1)

    def wait_body(t, _):
        for k in range(TOP_K):
            pltpu.make_async_copy(y_ref.at[pl.ds(0, 1), :], buf.at[slot, k, pl.ds(0, 1), :], sem.at[slot]).wait()
        return 0

    lax.fori_loop(0, tc, wait_body, 0, unroll=8)

    p = p_ref[...]
    y = p[:, 0:1] * buf[slot, 0]
    for k in range(1, TOP_K):
        y = y + p[:, k:k + 1] * buf[slot, k]
    o_ref[...] = _layer_norm(DN_ALPHA * h_ref[...] + y, g_ref[...], b_ref[...])


def _combine(dest_flat, y_rows, probs, h2, ln_g, ln_b, tc=128):
    T = h2.shape[0]
    nt = T // tc
    full = lambda shape: pl.BlockSpec(shape, lambda i: (0,) * len(shape))
    return pl.pallas_call(
        _combine_kernel,
        grid=(nt,),
        in_specs=[
            pl.BlockSpec((tc * TOP_K,), lambda i: (i,), memory_space=pltpu.SMEM),
            pl.BlockSpec((tc * TOP_K,), lambda i: (jnp.minimum(i + 1, nt - 1),), memory_space=pltpu.SMEM),
            pl.BlockSpec(memory_space=pl.ANY),
            pl.BlockSpec((tc, LANES), lambda i: (i, 0)),
            pl.BlockSpec((tc, D_MODEL), lambda i: (i, 0)),
            full((1, D_MODEL)), full((1, D_MODEL)),
        ],
        out_specs=pl.BlockSpec((tc, D_MODEL), lambda i: (i, 0)),
        out_shape=jax.ShapeDtypeStruct((T, D_MODEL), F32),
        scratch_shapes=[pltpu.VMEM((2, TOP_K, tc, D_MODEL), F32), pltpu.SemaphoreType.DMA((2,))],
        compiler_params=_cparams("arbitrary"),
        name="moe_combine",
    )(dest_flat, dest_flat, y_rows, probs, h2, ln_g, ln_b)


def _pad_lanes(v, fill=0.0):
    v = v.reshape(1, -1)
    return jnp.pad(v, ((0, 0), (0, LANES - v.shape[1])), constant_values=fill)


def _mixer_stage(x2, bsz, seq, ln_in_g, ln_in_b, w_in, b_branch_gate, conv_w, conv_b, dt_bias, a_log, d_skip,
                 ssd_norm_g, w_sb, w_ssd, w_mix_out, ln1_g, ln1_b):
    n_lin = 3 * D_MODEL + SSD_INNER + SSD_CONV_CH
    w_main = jnp.concatenate([w_in[:, :n_lin], w_in[:, n_lin + SSD_HEADS:]], axis=1).astype(BF16)
    w_dt = jnp.pad(w_in[:, n_lin:n_lin + SSD_HEADS], ((0, 0), (0, LANES - SSD_HEADS))).astype(BF16)
    h, qkv, z, xbc, gate_logits, dt_raw = _inproj(x2, ln_in_g.reshape(1, -1), ln_in_b.reshape(1, -1), w_main, w_dt)

    o_sb = _stick_breaking(qkv, bsz, seq)

    xbc_act = _conv_silu(xbc, conv_w, conv_b.reshape(1, -1), bsz, seq)
    a = -jnp.exp(a_log.astype(F32))
    o_ssd = _ssd(xbc_act, z, dt_raw, _pad_lanes(dt_bias), _pad_lanes(a),
                 jnp.repeat(d_skip, SSD_HEAD_DIM).reshape(1, -1), ssd_norm_g.reshape(1, -1), bsz, seq)

    return _merge(o_sb, o_ssd, gate_logits, h, b_branch_gate.reshape(1, -1), w_sb.astype(BF16), w_ssd.astype(BF16),
                  w_mix_out.astype(BF16), ln1_g.reshape(1, -1), ln1_b.reshape(1, -1))


def _xattn_stage(h1, mem2, bsz, seq, w_xq, w_xk, w_xv, w_xo, ln2_g, ln2_b):
    k, v = _kv_proj(mem2, w_xk.astype(BF16), w_xv.astype(BF16))
    return _xattn(h1, k, v, w_xq.astype(BF16), w_xo.astype(BF16), ln2_g.reshape(1, -1), ln2_b.reshape(1, -1),
                  bsz, seq)


def _moe_stage(h2, h2_packed, w_router, b_router, w_e_gate, b_e_gate, w_e_up, b_e_up, w_e_down, b_e_down, ln3_g,
               ln3_b):
    T = h2.shape[0]
    bm = EXPERT_ROWS
    w_r = jnp.pad(w_router, ((0, 0), (0, LANES - N_EXPERTS)))
    idx_p, probs, rank_p, counts_p = _router(h2, w_r, _pad_lanes(b_router))
    idx = idx_p[:, :TOP_K]
    rank = rank_p[:, :TOP_K]
    counts = counts_p[0, :N_EXPERTS].astype(jnp.int32)
    padded = (counts + bm - 1) // bm * bm
    end_padded = jnp.cumsum(padded)
    start_padded = end_padded - padded
    onehot = idx[:, :, None] == jnp.arange(N_EXPERTS, dtype=jnp.int32)[None, None, :]
    dest = jnp.sum(jnp.where(onehot, start_padded[None, None, :], 0), axis=-1) + rank
    dest_flat = dest.reshape(-1).astype(jnp.int32)
    n_blocks = -(-(T * TOP_K + N_EXPERTS * (bm - 1)) // bm)
    n_rows = n_blocks * bm
    blk_start = jnp.arange(n_blocks, dtype=jnp.int32) * bm
    block_e = jnp.minimum(jnp.sum(blk_start[:, None] >= end_padded[None, :], axis=-1), N_EXPERTS - 1).astype(jnp.int32)
    n_used = (end_padded[-1:] // bm).astype(jnp.int32)

    x_rows = _dispatch(dest_flat, h2_packed, n_rows)
    y_rows = _experts(block_e, n_used, x_rows, w_e_gate, b_e_gate[:, None, :], w_e_up, b_e_up[:, None, :],
                      w_e_down, b_e_down[:, None, :])
    return _combine(dest_flat, y_rows, probs, h2, ln3_g.reshape(1, -1), ln3_b.reshape(1, -1))


def kernel(x, mem, ln_in_g, ln_in_b, w_in, b_branch_gate, conv_w, conv_b, dt_bias, a_log, d_skip, ssd_norm_g, w_sb,
           w_ssd, w_mix_out, ln1_g, ln1_b, w_xq, w_xk, w_xv, w_xo, ln2_g, ln2_b, w_router, b_router, w_e_gate,
           b_e_gate, w_e_up, b_e_up, w_e_down, b_e_down, ln3_g, ln3_b):
    bsz, seq, _ = x.shape
    depth = w_in.shape[0]
    x2 = x.reshape(bsz * seq, D_MODEL)
    mem2 = mem.reshape(bsz * mem.shape[1], D_MODEL)
    assert depth == 1, "the entry LayerNorm is fused into the single layer's input projection"
    l = 0
    h1 = _mixer_stage(x2, bsz, seq, ln_in_g, ln_in_b, w_in[l], b_branch_gate[l], conv_w[l], conv_b[l], dt_bias[l],
                      a_log[l], d_skip[l], ssd_norm_g[l], w_sb[l], w_ssd[l], w_mix_out[l], ln1_g[l], ln1_b[l])
    h2, h2_packed = _xattn_stage(h1, mem2, bsz, seq, w_xq[l], w_xk[l], w_xv[l], w_xo[l], ln2_g[l], ln2_b[l])
    h = _moe_stage(h2, h2_packed, w_router[l], b_router[l], w_e_gate[l], b_e_gate[l], w_e_up[l], b_e_up[l], w_e_down[l],
                   b_e_down[l], ln3_g[l], ln3_b[l])
    return h.reshape(bsz, seq, D_MODEL)
```
